```python
import jax, jax.numpy as jnp
from jax import lax
import numpy as np

D_MODEL = 1024
BATCH = 8
SEQ = 4096
DEPTH = 4

GRID_W = 64
CTX_LEN = 256
N_MOD = 6
NORM_EPS = 1e-6
RWKV_HEAD_DIM = 64
RWKV_HEADS = D_MODEL // RWKV_HEAD_DIM
RWKV_DECAY_LORA = 64
RWKV_AAA_LORA = 64
RWKV_GATE_LORA = 128
RWKV_GN_EPS = 64e-5
ATTN_HEAD_DIM = 64
ATTN_Q_HEADS = D_MODEL // ATTN_HEAD_DIM
ATTN_KV_HEADS = 4
ATTN_GROUP = ATTN_Q_HEADS // ATTN_KV_HEADS
ATTN_BLOCK = 128
ROPE_THETA = 10000.0
ROPE_AXIS_DIM = ATTN_HEAD_DIM // 2
N_EXPERTS = 16
EC_CAPACITY = 2
D_EXPERT = 2816
F32 = jnp.float32

kernel_name = "hybrid_rwkv7_gqa_ecmoe_diffusion"


def rms_norm(x, gain):
    xf = x.astype(F32)
    y = xf * lax.rsqrt(jnp.mean(xf * xf, axis=-1, keepdims=True) + NORM_EPS)
    return (y * gain.astype(F32)).astype(x.dtype)


def centred_shift(h):
    hp = jnp.pad(h, ((0, 0), (1, 1), (0, 0)))
    return 0.5 * (hp[:, :-2] + hp[:, 2:])


def to_heads(z):
    return z.reshape(z.shape[:-1] + (RWKV_HEADS, RWKV_HEAD_DIM)).astype(F32)


def rwkv7_inputs(h, mu, w_rkv, w0, w1, w2, a0, a1, a2, g1, g2, k_k, k_a):
    xx = centred_shift(h) - h
    xs = h[None] + xx[None] * mu[:, None, None, :]
    r, k, v = jnp.einsum('nbtd,nde->nbte', xs[:3], w_rkv)
    w_pre = w0[:, None, None, :] + jnp.einsum('zbtr,zrd->zbtd', jnp.tanh(jnp.einsum('btd,zdr->zbtr', xs[3], w1)), w2)
    decay = jnp.exp(-jnp.exp(-jax.nn.softplus(-w_pre.astype(F32)) - 0.5))
    a = jax.nn.sigmoid((a0[:, None, None, :] + jnp.einsum('zbtr,zrd->zbtd', jnp.einsum('btd,zdr->zbtr', xs[4], a1), a2)).astype(F32))
    g = jnp.einsum('btr,rd->btd', jax.nn.sigmoid(jnp.einsum('btd,dr->btr', xs[5], g1)), g2)
    kk = to_heads(k * k_k)
    kk = kk / jnp.maximum(jnp.sqrt(jnp.sum(kk * kk, axis=-1, keepdims=True)), 1e-12)
    k_dir = k[None].astype(F32) * (1.0 + (a - 1.0) * k_a.astype(F32))
    return to_heads(r), to_heads(decay), to_heads(k_dir), to_heads(v), kk, to_heads(a), g


def rwkv7_scan(s0, r, w, k, v, kk, a, reverse):
    def step(S, inp):
        r_t, w_t, k_t, v_t, kk_t, a_t = inp
        sa = -jnp.einsum('bhvk,bhk->bhv', S, kk_t)
        S = S * w_t[:, :, None, :] + sa[..., None] * (kk_t * a_t)[:, :, None, :] + v_t[..., None] * k_t[:, :, None, :]
        return S, jnp.einsum('bhvk,bhk->bhv', S, r_t)
    xs = tuple(jnp.moveaxis(z, 1, 0) for z in (r, w, k, v, kk, a))
    s_final, y = lax.scan(step, s0, xs, reverse=reverse)
    return s_final, jnp.moveaxis(y, 0, 1)


def rwkv7_readout(y, r, k_dir, v, g, r_k, ln_w, ln_b, w_o):
    B, T = y.shape[:2]
    mean = jnp.mean(y, axis=-1, keepdims=True)
    var = jnp.mean(jnp.square(y - mean), axis=-1, keepdims=True)
    yn = ((y - mean) * lax.rsqrt(var + RWKV_GN_EPS)).reshape(B, T, D_MODEL) * ln_w.astype(F32) + ln_b.astype(F32)
    bonus = jnp.sum(r[None] * k_dir * r_k.astype(F32), axis=(0, -1))[..., None] * v
    out = (yn + bonus.reshape(B, T, D_MODEL)) * g.astype(F32)
    return jnp.einsum('btd,de->bte', out.astype(w_o.dtype), w_o)


def rwkv7_mixer(h_lat, h_ctx, mu, w_rkv, w0, w1, w2, a0, a1, a2, g1, g2, k_k, k_a, r_k, ln_w, ln_b, w_o, need_ctx):
    r_l, w_l, k_l, v_l, kk_l, a_l, g_l = rwkv7_inputs(h_lat, mu, w_rkv, w0, w1, w2, a0, a1, a2, g1, g2, k_k, k_a)
    r_c, w_c, k_c, v_c, kk_c, a_c, g_c = rwkv7_inputs(h_ctx, mu, w_rkv, w0, w1, w2, a0, a1, a2, g1, g2, k_k, k_a)
    s0 = jnp.zeros((h_lat.shape[0], RWKV_HEADS, RWKV_HEAD_DIM, RWKV_HEAD_DIM), F32)
    y_l = jnp.zeros_like(r_l)
    y_c = jnp.zeros_like(r_c)
    for d, rev in enumerate((False, True)):
        s_c, yc = rwkv7_scan(s0, r_c, w_c[d], k_c[d], v_c, kk_c, a_c[d], rev)
        _, yl = rwkv7_scan(s_c, r_l, w_l[d], k_l[d], v_l, kk_l, a_l[d], rev)
        y_l = y_l + yl
        y_c = y_c + yc
    out_l = rwkv7_readout(y_l, r_l, k_l, v_l, g_l, r_k, ln_w, ln_b, w_o)
    out_c = rwkv7_readout(y_c, r_c, k_c, v_c, g_c, r_k, ln_w, ln_b, w_o) if need_ctx else None
    return out_l, out_c


def axial_rope_angles(T):
    rows = T // GRID_W
    row = jnp.broadcast_to(jnp.arange(rows, dtype=F32)[:, None], (rows, GRID_W)).reshape(-1)
    col = jnp.broadcast_to(jnp.arange(GRID_W, dtype=F32)[None, :], (rows, GRID_W)).reshape(-1)
    inv_freq = ROPE_THETA ** (-jnp.arange(0, ROPE_AXIS_DIM, 2, dtype=F32) / ROPE_AXIS_DIM)
    ang = jnp.stack([row, col], axis=-1)[:, :, None] * inv_freq
    return jnp.cos(ang), jnp.sin(ang)


def apply_axial_rope(x, cos, sin):
    B, T, H, _ = x.shape
    xf = x.astype(F32).reshape(B, T, H, 2, 2, ROPE_AXIS_DIM // 2)
    x1, x2 = xf[..., 0, :], xf[..., 1, :]
    c, s = cos[:, None], sin[:, None]
    out = jnp.stack([x1 * c - x2 * s, x1 * s + x2 * c], axis=-2)
    return out.reshape(B, T, H, ATTN_HEAD_DIM).astype(x.dtype)


def attend(q, k, v):
    s = jnp.einsum('bkgqd,bksd->bkgqs', q, k).astype(F32) * (ATTN_HEAD_DIM ** -0.5)
    p = jax.nn.softmax(s, axis=-1).astype(v.dtype)
    return jnp.einsum('bkgqs,bksd->bkgqd', p, v)


def gqa_project(h, w_qkv, q_gain, k_gain):
    B, T, _ = h.shape
    qkv = jnp.einsum('btd,de->bte', h, w_qkv)
    q, k, v = jnp.split(qkv, [ATTN_Q_HEADS * ATTN_HEAD_DIM, (ATTN_Q_HEADS + ATTN_KV_HEADS) * ATTN_HEAD_DIM], axis=-1)
    q = rms_norm(q.reshape(B, T, ATTN_Q_HEADS, ATTN_HEAD_DIM), q_gain)
    k = rms_norm(k.reshape(B, T, ATTN_KV_HEADS, ATTN_HEAD_DIM), k_gain)
    return q, k, v.reshape(B, T, ATTN_KV_HEADS, ATTN_HEAD_DIM)


def group_q(q):
    B, T = q.shape[:2]
    return q.reshape(B, T, ATTN_KV_HEADS, ATTN_GROUP, ATTN_HEAD_DIM).transpose(0, 2, 3, 1, 4)


def ungroup(o):
    B, T = o.shape[0], o.shape[3]
    return o.transpose(0, 3, 1, 2, 4).reshape(B, T, ATTN_Q_HEADS * ATTN_HEAD_DIM)


def gqa_mixer(h_lat, h_ctx, w_qkv, q_gain, k_gain, w_o, need_ctx):
    B, T, _ = h_lat.shape
    q_l, k_l, v_l = gqa_project(h_lat, w_qkv, q_gain, k_gain)
    q_c, k_c, v_c = gqa_project(h_ctx, w_qkv, q_gain, k_gain)
    cos, sin = axial_rope_angles(T)
    q_l = apply_axial_rope(q_l, cos, sin)
    k_l = apply_axial_rope(k_l, cos, sin)
    k_all = jnp.concatenate([k_c, k_l], axis=1).transpose(0, 2, 1, 3)
    v_all = jnp.concatenate([v_c, v_l], axis=1).transpose(0, 2, 1, 3)
    nb = T // ATTN_BLOCK
    qb = group_q(q_l).reshape(B, ATTN_KV_HEADS, ATTN_GROUP, nb, ATTN_BLOCK, ATTN_HEAD_DIM).transpose(3, 0, 1, 2, 4, 5)
    o = lax.map(lambda blk: attend(blk, k_all, v_all), qb)
    o = o.transpose(1, 2, 3, 0, 4, 5).reshape(B, ATTN_KV_HEADS, ATTN_GROUP, T, ATTN_HEAD_DIM)
    out_l = jnp.einsum('bte,ed->btd', ungroup(o), w_o)
    out_c = None
    if need_ctx:
        oc = attend(group_q(q_c), k_c.transpose(0, 2, 1, 3), v_c.transpose(0, 2, 1, 3))
        out_c = jnp.einsum('bte,ed->btd', ungroup(oc), w_o)
    return out_l, out_c


def ec_moe(h, w_router, w_gate_up, w_down):
    B, N, D = h.shape
    cap = EC_CAPACITY * N // N_EXPERTS
    aff = jax.nn.softmax(jnp.einsum('bnd,de->bne', h, w_router).astype(F32), axis=-1)
    gate, idx = lax.top_k(jnp.swapaxes(aff, 1, 2), cap)
    xe = jax.vmap(lambda hb, ib: hb[ib])(h, idx)
    gp, up = jnp.split(jnp.einsum('becd,edf->becf', xe, w_gate_up), 2, axis=-1)
    ye = jnp.einsum('becf,efd->becd', jax.nn.silu(gp) * up, w_down) * gate[..., None].astype(h.dtype)
    return jax.vmap(lambda ib, yb: jnp.zeros((N, D), yb.dtype).at[ib.reshape(-1)].add(yb.reshape(-1, D)))(idx, ye)


def setup_inputs(seed: int = 0) -> dict:
    key = jax.random.key(seed)
    ks = iter(jax.random.split(key, 40))
    D = D_MODEL
    na = (DEPTH + 1) // 2
    nb = DEPTH // 2
    qkv_w = (ATTN_Q_HEADS + 2 * ATTN_KV_HEADS) * ATTN_HEAD_DIM

    def nrm(shape, scale):
        return jax.random.normal(next(ks), shape, F32) * scale

    return {
        "x": nrm((BATCH, SEQ, D), 1.0),
        "c": nrm((BATCH, D), 1.0),
        "ctx": nrm((BATCH, CTX_LEN, D), 1.0),
        "c_ctx": nrm((D,), 1.0),
        "mod_w": nrm((DEPTH, D, N_MOD * D), 0.5 * D ** -0.5),
        "mod_b": nrm((DEPTH, N_MOD * D), 0.01),
        "norm_mix": 1.0 + nrm((DEPTH, D), 0.05),
        "norm_ffn": 1.0 + nrm((DEPTH, D), 0.05),
        "rwkv_mu": jax.random.uniform(next(ks), (na, 6, D), F32),
        "rwkv_w_rkv": nrm((na, 3, D, D), D ** -0.5),
        "rwkv_w0": nrm((na, 2, D), 0.5) - 1.0,
        "rwkv_w1": nrm((na, 2, D, RWKV_DECAY_LORA), D ** -0.5),
        "rwkv_w2": nrm((na, 2, RWKV_DECAY_LORA, D), 0.5 * RWKV_DECAY_LORA ** -0.5),
        "rwkv_a0": nrm((na, 2, D), 0.5),
        "rwkv_a1": nrm((na, 2, D, RWKV_AAA_LORA), D ** -0.5),
        "rwkv_a2": nrm((na, 2, RWKV_AAA_LORA, D), 0.5 * RWKV_AAA_LORA ** -0.5),
        "rwkv_g1": nrm((na, D, RWKV_GATE_LORA), D ** -0.5),
        "rwkv_g2": nrm((na, RWKV_GATE_LORA, D), RWKV_GATE_LORA ** -0.5),
        "rwkv_k_k": 0.85 + nrm((na, D), 0.05),
        "rwkv_k_a": 1.0 + nrm((na, D), 0.05),
        "rwkv_r_k": nrm((na, RWKV_HEADS, RWKV_HEAD_DIM), 0.1),
        "rwkv_ln_w": 1.0 + nrm((na, D), 0.05),
        "rwkv_ln_b": nrm((na, D), 0.01),
        "rwkv_w_o": nrm((na, D, D), D ** -0.5),
        "attn_w_qkv": nrm((nb, D, qkv_w), D ** -0.5),
        "attn_q_gain": 1.0 + nrm((nb, ATTN_HEAD_DIM), 0.05),
        "attn_k_gain": 1.0 + nrm((nb, ATTN_HEAD_DIM), 0.05),
        "attn_w_o": nrm((nb, ATTN_Q_HEADS * ATTN_HEAD_DIM, D), (ATTN_Q_HEADS * ATTN_HEAD_DIM) ** -0.5),
        "moe_router": nrm((DEPTH, D, N_EXPERTS), D ** -0.5),
        "moe_w_gate_up": nrm((DEPTH, N_EXPERTS, D, 2 * D_EXPERT), D ** -0.5),
        "moe_w_down": nrm((DEPTH, N_EXPERTS, D_EXPERT, D), D_EXPERT ** -0.5),
        "final_norm": 1.0 + nrm((D,), 0.05),
    }


def reference(x, c, ctx, c_ctx, mod_w, mod_b, norm_mix, norm_ffn,
              rwkv_mu, rwkv_w_rkv, rwkv_w0, rwkv_w1, rwkv_w2, rwkv_a0, rwkv_a1, rwkv_a2,
              rwkv_g1, rwkv_g2, rwkv_k_k, rwkv_k_a, rwkv_r_k, rwkv_ln_w, rwkv_ln_b, rwkv_w_o,
              attn_w_qkv, attn_q_gain, attn_k_gain, attn_w_o,
              moe_router, moe_w_gate_up, moe_w_down, final_norm):
    B = x.shape[0]
    ia = 0
    ib = 0
    for i in range(DEPTH):
        need_ctx = i < DEPTH - 1
        m_l = (jnp.einsum('bd,de->be', jax.nn.silu(c), mod_w[i]) + mod_b[i]).reshape(B, N_MOD, 1, D_MODEL)
        m_c = (jnp.einsum('d,de->e', jax.nn.silu(c_ctx), mod_w[i]) + mod_b[i]).reshape(N_MOD, D_MODEL)
        h_l = rms_norm(x, norm_mix[i]) * (1.0 + m_l[:, 1]) + m_l[:, 0]
        h_c = rms_norm(ctx, norm_mix[i]) * (1.0 + m_c[1]) + m_c[0]
        if i % 2 == 0:
            o_l, o_c = rwkv7_mixer(h_l, h_c, rwkv_mu[ia], rwkv_w_rkv[ia], rwkv_w0[ia], rwkv_w1[ia], rwkv_w2[ia],
                                   rwkv_a0[ia], rwkv_a1[ia], rwkv_a2[ia], rwkv_g1[ia], rwkv_g2[ia],
                                   rwkv_k_k[ia], rwkv_k_a[ia], rwkv_r_k[ia], rwkv_ln_w[ia], rwkv_ln_b[ia],
                                   rwkv_w_o[ia], need_ctx)
            ia += 1
        else:
            o_l, o_c = gqa_mixer(h_l, h_c, attn_w_qkv[ib], attn_q_gain[ib], attn_k_gain[ib], attn_w_o[ib], need_ctx)
            ib += 1
        x = x + m_l[:, 2] * o_l
        f_l = rms_norm(x, norm_ffn[i]) * (1.0 + m_l[:, 4]) + m_l[:, 3]
        x = x + m_l[:, 5] * ec_moe(f_l, moe_router[i], moe_w_gate_up[i], moe_w_down[i])
        if need_ctx:
            ctx = ctx + m_c[2] * o_c
            f_c = rms_norm(ctx, norm_ffn[i]) * (1.0 + m_c[4]) + m_c[3]
            ctx = ctx + m_c[5] * ec_moe(f_c, moe_router[i], moe_w_gate_up[i], moe_w_down[i])
    return rms_norm(x, final_norm)
```

```python
import functools

import jax
import jax.numpy as jnp
from jax import lax
from jax.experimental import pallas as pl
from jax.experimental.pallas import tpu as pltpu

F32 = jnp.float32
BF16 = jnp.bfloat16

HEAD_DIM = 64
N_MOD = 6
NORM_EPS = 1e-6
GN_EPS = 64e-5
ROPE_THETA = 10000.0
GRID_W = 64
ATTN_KV_HEADS = 4
N_EXPERTS = 16
EC_CAPACITY = 2
TM = 256
CHUNK = 64
LANES = 128
VMEM_LIMIT = 56 * 1024 * 1024
NEG_EXP_M05 = -0.6065306597126334


def _cparams(*sem):
    return pltpu.CompilerParams(dimension_semantics=sem, vmem_limit_bytes=VMEM_LIMIT)


def _split2(x):
    hi = x.astype(BF16)
    lo = (x - hi.astype(F32)).astype(BF16)
    return hi, lo


def _split3(x):
    hi = x.astype(BF16)
    r1 = x - hi.astype(F32)
    mid = r1.astype(BF16)
    lo = (r1 - mid.astype(F32)).astype(BF16)
    return hi, mid, lo


def _dot(a, b):
    return jnp.dot(a, b, preferred_element_type=F32)


def _dot_nt(a, b):
    return lax.dot_general(a, b, (((1,), (1,)), ((), ())), preferred_element_type=F32)


def _dot_tn(a, b):
    return lax.dot_general(a, b, (((0,), (0,)), ((), ())), preferred_element_type=F32)


def _bdot(a, b):
    return _dot(a.astype(BF16), b.astype(BF16))


def _mm(a, b, passes, kind="nn"):
    f = {"nn": _dot, "nt": _dot_nt, "tn": _dot_tn}[kind]
    if passes == 1:
        return f(a.astype(BF16), b.astype(BF16))
    ah, al = _split2(a)
    bh, bl = _split2(b)
    return f(ah, bh) + (f(ah, bl) + f(al, bh))


def _dot_exact_rhs(a, b_bf16):
    h, m, l = _split3(a)
    return _dot(h, b_bf16) + (_dot(m, b_bf16) + _dot(l, b_bf16))


def _seg_sum(x, seg, segt):
    s = _dot_exact_rhs(x, seg)
    return _dot_exact_rhs(s, segt)


def _norm_mod(x, gain, shift, scale):
    ms = jnp.mean(x * x, axis=-1, keepdims=True)
    y = x * lax.rsqrt(ms + NORM_EPS)
    return (y * gain) * (1.0 + scale) + shift


def _sigmoid(x):
    return 1.0 / (1.0 + jnp.exp(-x))


def _silu(x):
    return x * _sigmoid(x)


def _mod_row(i, tiles_per_sample):
    return jnp.where(i % tiles_per_sample == 0, 0, 1 + i // tiles_per_sample)


def _const_spec(shape):
    nd = len(shape)
    return pl.BlockSpec(shape, lambda *_: (0,) * nd)


def _mod_kernel(c_ref, w_ref, b_ref, o_ref):
    s = _silu(c_ref[...])
    o_ref[0] = _mm(s, w_ref[0], 3) + b_ref[0]


def _mod_vectors(cc, mod_w, mod_b):
    depth, d, n = mod_w.shape
    tn = 1536
    return pl.pallas_call(
        _mod_kernel,
        out_shape=jax.ShapeDtypeStruct((depth, 16, n), F32),
        grid=(depth, n // tn),
        in_specs=[
            pl.BlockSpec((16, d), lambda l, j: (0, 0)),
            pl.BlockSpec((1, d, tn), lambda l, j: (l, 0, j)),
            pl.BlockSpec((1, 1, tn), lambda l, j: (l, 0, j)),
        ],
        out_specs=pl.BlockSpec((1, 16, tn), lambda l, j: (l, 0, j)),
        compiler_params=_cparams("arbitrary", "arbitrary"),
        name="mod_vectors",
    )(cc, mod_w, mod_b.reshape(depth, 1, n))


def _rwkv_in_kernel(x_ref, xp_ref, xn_ref, m_ref, gain_ref, mu_ref, wrkv_ref, w1_ref, w2_ref, w0_ref,
                    a1_ref, a2_ref, a0_ref, g1_ref, g2_ref, kk_ref_, ka_ref, seg_ref, segt_ref,
                    r_o, v_o, kk_o, g_o, lw_o, kd_o, b_o, *, tiles_per_sample):
    i = pl.program_id(0)
    j = i % tiles_per_sample
    m = m_ref[0]
    shift, scale = m[0:1], m[1:2]
    gain = gain_ref[...]
    h = _norm_mod(x_ref[...], gain, shift, scale)
    has_prev = (j >= 2).astype(F32)
    has_next = jnp.logical_and(j >= 1, j <= tiles_per_sample - 2).astype(F32)
    hp_row = _norm_mod(xp_ref[7:8, :], gain, shift, scale) * has_prev
    hn_row = _norm_mod(xn_ref[0:1, :], gain, shift, scale) * has_next
    row = lax.broadcasted_iota(jnp.int32, h.shape, 0)
    h_prev = jnp.where(row == 0, hp_row, pltpu.roll(h, 1, 0))
    h_next = jnp.where(row == TM - 1, hn_row, pltpu.roll(h, TM - 1, 0))
    xx = 0.5 * (h_prev + h_next) - h
    mu = mu_ref[...]

    def mix(n):
        return (h + xx * mu[n:n + 1]).astype(BF16)

    r = _dot(mix(0), wrkv_ref[0])
    k = _dot(mix(1), wrkv_ref[1])
    v = _dot(mix(2), wrkv_ref[2])
    tw = jnp.tanh(_dot(mix(3), w1_ref[...])).astype(BF16)
    ua = _dot(mix(4), a1_ref[...]).astype(BF16)
    g = _dot(_sigmoid(_dot(mix(5), g1_ref[...])).astype(BF16), g2_ref[...])
    kk = k * kk_ref_[...]
    n2 = _seg_sum(kk * kk, seg_ref[...], segt_ref[...])
    kk = kk / jnp.maximum(jnp.sqrt(n2), 1e-12)
    r_o[...] = r
    v_o[...] = v
    kk_o[...] = kk
    g_o[...] = g
    ka = ka_ref[...]
    for z in range(2):
        w_pre = w0_ref[z:z + 1, :] + _dot(tw, w2_ref[z])
        lw_o[z] = NEG_EXP_M05 * _sigmoid(w_pre)
        a =_sigmoid(a0_ref[z:z + 1, :] + _dot(ua, a2_ref[z]))
        kd_o[z] = k * (1.0 + (a - 1.0) * ka)
        b_o[z] = kk * a


def _rwkv_inputs(x2, mod, gain, p, seg, segt, tiles_per_sample):
    n, d = x2.shape
    nt = n // TM
    blk8 = TM // 8
    last8 = n // 8 - 1
    row_spec = pl.BlockSpec((TM, d), lambda i: (i, 0))
    dir_spec = pl.BlockSpec((2, TM, d), lambda i: (0, i, 0))
    tok = jax.ShapeDtypeStruct((n, d), F32)
    tok2 = jax.ShapeDtypeStruct((2, n, d), F32)
    return pl.pallas_call(
        functools.partial(_rwkv_in_kernel, tiles_per_sample=tiles_per_sample),
        out_shape=(tok, tok, tok, tok, tok2, tok2, tok2),
        grid=(nt,),
        in_specs=[
            row_spec,
            pl.BlockSpec((8, d), lambda i: (jnp.maximum(i * blk8 - 1, 0), 0)),
            pl.BlockSpec((8, d), lambda i: (jnp.minimum((i + 1) * blk8, last8), 0)),
            pl.BlockSpec((1, N_MOD, d), lambda i: (_mod_row(i, tiles_per_sample), 0, 0)),
            _const_spec((1, d)),
            _const_spec((6, d)),
            _const_spec((3, d, d)),
            _const_spec((d, LANES)),
            _const_spec((2, LANES, d)),
            _const_spec((2, d)),
            _const_spec((d, LANES)),
            _const_spec((2, LANES, d)),
            _const_spec((2, d)),
            _const_spec((d, LANES)),
            _const_spec((LANES, d)),
            _const_spec((1, d)),
            _const_spec((1, d)),
            _const_spec((d, LANES)),
            _const_spec((LANES, d)),
        ],
        out_specs=(row_spec, row_spec, row_spec, row_spec, dir_spec, dir_spec, dir_spec),
        compiler_params=_cparams("arbitrary"),
        name="rwkv_inputs",
    )(x2, x2, x2, mod, gain, p["mu"], p["w_rkv"], p["w1"], p["w2"], p["w0"], p["a1"], p["a2"], p["a0"],
      p["g1"], p["g2"], p["k_k"], p["k_a"], seg, segt)


def _scan_pair(r, v, kk, lw, kd, b, m0, c, passes):
    cl = _dot_exact_rhs_left(c["tri"], lw)
    tot = jnp.sum(lw, axis=0, keepdims=True)
    e_cl = jnp.exp(cl)
    e_ncl = jnp.exp(-cl)
    kk_g = kk * jnp.exp(cl - lw)
    r_g = r * e_cl
    k_i = kd * e_ncl
    b_i = b * e_ncl
    e_end = jnp.exp(tot - cl)
    k_e = kd * e_end
    b_e = b * e_end

    def stack(x):
        return jnp.concatenate([jnp.where(c["head0"], x, 0.0), jnp.where(c["head0"], 0.0, x)], axis=0)

    n2 = 2 * CHUNK
    q2 = jnp.concatenate([stack(kk_g), stack(r_g)], axis=0)
    k2 = jnp.concatenate([stack(k_i), stack(b_i)], axis=0)
    a_all = _mm(q2, k2, passes, "nt")
    l_kk = jnp.where(c["strict"], a_all[:n2, :n2], 0.0)
    l_kb = jnp.where(c["strict"], a_all[:n2, n2:], 0.0)
    l_rk = jnp.where(c["incl"], a_all[n2:, :n2], 0.0)
    l_rb = jnp.where(c["incl"], a_all[n2:, n2:], 0.0)
    pw = -l_kb
    inv = c["eye"] + pw
    steps = CHUNK.bit_length() - 2
    for _ in range(steps):
        pw = _mm(pw, pw, passes)
        inv = inv + _mm(inv, pw, passes)
    qm = _mm(q2, m0, passes)
    vs = stack(v)
    us = _mm(inv, qm[:n2] + _mm(l_kk, vs, passes), passes)
    ys = qm[n2:] + _mm(jnp.concatenate([l_rk, -l_rb], axis=1), jnp.concatenate([vs, us], axis=0), passes)
    y = ys[:CHUNK] + ys[CHUNK:]
    u = us[:CHUNK] + us[CHUNK:]
    upd = _mm(jnp.concatenate([k_e, -b_e], axis=0).T, jnp.concatenate([v, u], axis=0), passes)
    g_col = jnp.sum(jnp.where(c["eye"] > 0, jnp.exp(tot), 0.0), axis=1, keepdims=True)
    m1 = m0 * g_col + jnp.where(c["same"], upd, 0.0)
    return y, m1


def _dot_exact_rhs_left(a_bf16, b):
    h, m, l = _split3(b)
    return _dot(a_bf16, h) + (_dot(a_bf16, m) + _dot(a_bf16, l))


def _scan_kernel(r_ref, v_ref, kk_ref, lw_ref, kd_ref, b_ref, y_ref, m_ref, *, pairs, passes):
    z = pl.program_id(0)
    ci = pl.program_id(3)

    @pl.when(ci == 0)
    def _():
        m_ref[...] = jnp.zeros_like(m_ref)

    sgn = 1 - 2 * z
    n2 = 2 * CHUNK
    row = lax.broadcasted_iota(jnp.int32, (n2, n2), 0)
    col = lax.broadcasted_iota(jnp.int32, (n2, n2), 1)
    same = (row // CHUNK) == (col // CHUNK)
    dt = (row % CHUNK - col % CHUNK) * sgn
    rc = lax.broadcasted_iota(jnp.int32, (CHUNK, CHUNK), 0)
    cc = lax.broadcasted_iota(jnp.int32, (CHUNK, CHUNK), 1)
    consts = {
        "same": same,
        "strict": jnp.logical_and(same, dt > 0),
        "incl": jnp.logical_and(same, dt >= 0),
        "eye": (row == col).astype(F32),
        "tri": ((rc - cc) * sgn >= 0).astype(BF16),
        "head0": lax.broadcasted_iota(jnp.int32, (CHUNK, LANES), 1) < HEAD_DIM,
    }
    for p in range(pairs):
        sl = slice(p * LANES, (p + 1) * LANES)
        y, m1 = _scan_pair(r_ref[:, sl], v_ref[:, sl], kk_ref[:, sl], lw_ref[0, :, sl], kd_ref[0, :, sl],
                           b_ref[0, :, sl], m_ref[p], consts, passes)
        y_ref[0, :, sl] = y
        m_ref[p] = m1


def _rwkv_scan(r, v, kk, lw, kd, b, batch, ctx_len, pairs=4, passes=3):
    n, d = r.shape
    tt = n // batch
    nch = tt // CHUNK
    nch_ctx = ctx_len // CHUNK
    width = pairs * LANES
    groups = d // width

    def chunk_row(z, bb, c):
        rev = jnp.where(c < nch_ctx, nch_ctx - 1 - c, nch + nch_ctx - 1 - c)
        return bb * nch + jnp.where(z == 0, c, rev)

    shared = pl.BlockSpec((CHUNK, width), lambda z, bb, g, c: (chunk_row(z, bb, c), g))
    per_dir = pl.BlockSpec((1, CHUNK, width), lambda z, bb, g, c: (z, chunk_row(z, bb, c), g))
    return pl.pallas_call(
        functools.partial(_scan_kernel, pairs=pairs, passes=passes),
        out_shape=jax.ShapeDtypeStruct((2, n, d), F32),
        grid=(2, batch, groups, nch),
        in_specs=[shared, shared, shared, per_dir, per_dir, per_dir],
        out_specs=per_dir,
        scratch_shapes=[pltpu.VMEM((pairs, LANES, LANES), F32)],
        compiler_params=_cparams("arbitrary", "arbitrary", "arbitrary", "arbitrary"),
        name="rwkv_scan",
    )(r, v, kk, lw, kd, b)


def _rwkv_out_kernel(y0_ref, y1_ref, r_ref, kd0_ref, kd1_ref, v_ref, g_ref, x_ref, m_ref, rk_ref, lnw_ref,
                     lnb_ref, wo_ref, seg_ref, segt_ref, o_ref):
    seg, segt = seg_ref[...], segt_ref[...]
    y = y0_ref[0] + y1_ref[0]
    mean = _seg_sum(y, seg, segt) * (1.0 / HEAD_DIM)
    dy = y - mean
    var = _seg_sum(dy * dy, seg, segt) * (1.0 / HEAD_DIM)
    yn = (dy * lax.rsqrt(var + GN_EPS)) * lnw_ref[...] + lnb_ref[...]
    bonus = _seg_sum(r_ref[...] * (kd0_ref[0] + kd1_ref[0]) * rk_ref[...], seg, segt)
    out = (yn + bonus * v_ref[...]) * g_ref[...]
    o = _dot(out.astype(BF16), wo_ref[...])
    o_ref[...] = x_ref[...] + m_ref[0][2:3] * o


def _rwkv_readout(y, r, kd, v, g, x2, mod, p, seg, segt, tiles_per_sample):
    n, d = x2.shape
    row_spec = pl.BlockSpec((TM, d), lambda i: (i, 0))
    return pl.pallas_call(
        _rwkv_out_kernel,
        out_shape=jax.ShapeDtypeStruct((n, d), F32),
        grid=(n // TM,),
        in_specs=[
            pl.BlockSpec((1, TM, d), lambda i: (0, i, 0)),
            pl.BlockSpec((1, TM, d), lambda i: (1, i, 0)),
            row_spec,
            pl.BlockSpec((1, TM, d), lambda i: (0, i, 0)),
            pl.BlockSpec((1, TM, d), lambda i: (1, i, 0)),
            row_spec, row_spec, row_spec,
            pl.BlockSpec((1, N_MOD, d), lambda i: (_mod_row(i, tiles_per_sample), 0, 0)),
            _const_spec((1, d)), _const_spec((1, d)), _const_spec((1, d)),
            _const_spec((d, d)),
            _const_spec((d, LANES)), _const_spec((LANES, d)),
        ],
        out_specs=row_spec,
        compiler_params=_cparams("arbitrary"),
        name="rwkv_readout",
    )(y, y, r, kd, kd, v, g, x2, mod, p["r_k"], p["ln_w"], p["ln_b"], p["w_o"], seg, segt)


def _rope(x, cos, sin_lo, sin_hi):
    w = x.shape[1]
    reps = w // LANES
    tile = lambda t: jnp.concatenate([t] * reps, axis=1) if reps > 1 else t
    half = HEAD_DIM // 4
    return x * tile(cos) + pltpu.roll(x, w - half, 1) * tile(sin_lo) + pltpu.roll(x, half, 1) * tile(sin_hi)


def _gqa_proj_kernel(x_ref, m_ref, gain_ref, w_ref, qg_ref, kg_ref, cos_ref, slo_ref, shi_ref, seg_ref, segt_ref,
                     segk_ref, segtk_ref, q_o, k_o, v_o, *, d, dkv):
    m = m_ref[0]
    h = _norm_mod(x_ref[...], gain_ref[...], m[0:1], m[1:2]).astype(BF16)
    qkv = _dot(h, w_ref[...])
    q, k, v = qkv[:, :d], qkv[:, d:d + dkv], qkv[:, d + dkv:]
    cos, slo, shi = cos_ref[...], slo_ref[...], shi_ref[...]
    qms = _seg_sum(q * q, seg_ref[...], segt_ref[...]) * (1.0 / HEAD_DIM)
    q = (q * lax.rsqrt(qms + NORM_EPS)) * qg_ref[...]
    kms = _seg_sum(k * k, segk_ref[...], segtk_ref[...]) * (1.0 / HEAD_DIM)
    k = (k * lax.rsqrt(kms + NORM_EPS)) * kg_ref[...]
    q = (_rope(q, cos, slo, shi) * (HEAD_DIM ** -0.5)).astype(BF16)
    k = _rope(k, cos, slo, shi).astype(BF16)
    v = v.astype(BF16)
    for hh in range(d // HEAD_DIM):
        q_o[hh] = q[:, hh * HEAD_DIM:(hh + 1) * HEAD_DIM]
    for hh in range(dkv // HEAD_DIM):
        k_o[hh] = k[:, hh * HEAD_DIM:(hh + 1) * HEAD_DIM]
        v_o[hh] = v[:, hh * HEAD_DIM:(hh + 1) * HEAD_DIM]


def _gqa_project(x2, mod, gain, p, rope, seg, segt, segk, segtk, tiles_per_sample):
    n, d = x2.shape
    dkv = ATTN_KV_HEADS * HEAD_DIM
    nq, nkv = d // HEAD_DIM, ATTN_KV_HEADS
    heads = lambda h: pl.BlockSpec((h, TM, HEAD_DIM), lambda i: (0, i, 0))
    pos = pl.BlockSpec((TM, LANES), lambda i: (i % tiles_per_sample, 0))
    return pl.pallas_call(
        functools.partial(_gqa_proj_kernel, d=d, dkv=dkv),
        out_shape=(jax.ShapeDtypeStruct((nq, n, HEAD_DIM), BF16), jax.ShapeDtypeStruct((nkv, n, HEAD_DIM), BF16),
                   jax.ShapeDtypeStruct((nkv, n, HEAD_DIM), BF16)),
        grid=(n // TM,),
        in_specs=[
            pl.BlockSpec((TM, d), lambda i: (i, 0)),
            pl.BlockSpec((1, N_MOD, d), lambda i: (_mod_row(i, tiles_per_sample), 0, 0)),
            _const_spec((1, d)),
            _const_spec((d, d + 2 * dkv)),
            _const_spec((1, d)), _const_spec((1, dkv)),
            pos, pos, pos,
            _const_spec((d, LANES)), _const_spec((LANES, d)),
            _const_spec((dkv, LANES)), _const_spec((LANES, dkv)),
        ],
        out_specs=(heads(nq), heads(nkv), heads(nkv)),
        compiler_params=_cparams("arbitrary"),
        name="gqa_project",
    )(x2, mod, gain, p["w_qkv"], p["q_gain"], p["k_gain"], rope[0], rope[1], rope[2], seg, segt, segk, segtk)


def _attn_kernel(q_ref, k_ref, v_ref, *rest, group):
    o_ref = rest[-1]
    k = k_ref[0]
    v = v_ref[0]
    for hh in range(group):
        s = _dot_nt(q_ref[hh], k)
        mx = jnp.max(s, axis=-1, keepdims=True)
        pr = jnp.exp(s - mx)
        den = jnp.sum(pr, axis=-1, keepdims=True)
        o_ref[hh] = (_dot(pr.astype(BF16), v) / den).astype(BF16)


def _attention(q, k, v, batch, tiles_per_sample, q_tiles, q_tile0, kv_rows, into=None):
    nq, n, _ = q.shape
    nkv = k.shape[0]
    group = nq // nkv
    tt = n // batch
    assert tt % kv_rows == 0
    q_spec = pl.BlockSpec((group, TM, HEAD_DIM), lambda bb, g, t: (g, bb * tiles_per_sample + q_tile0 + t, 0))
    kv_spec = pl.BlockSpec((1, kv_rows, HEAD_DIM), lambda bb, g, t: (g, bb * (tt // kv_rows), 0))
    in_specs = [q_spec, kv_spec, kv_spec]
    args = [q, k, v]
    aliases = {}
    if into is not None:
        in_specs.append(pl.BlockSpec(memory_space=pl.ANY))
        args.append(into)
        aliases = {3: 0}
    return pl.pallas_call(
        functools.partial(_attn_kernel, group=group),
        out_shape=jax.ShapeDtypeStruct((nq, n, HEAD_DIM), BF16),
        grid=(batch, nkv, q_tiles),
        in_specs=in_specs,
        out_specs=q_spec,
        input_output_aliases=aliases,
        compiler_params=_cparams("arbitrary", "arbitrary", "arbitrary"),
        name="gqa_attention",
    )(*args)


def _attn_out_kernel(o_ref, x_ref, m_ref, wo_ref, y_ref):
    acc = _dot(o_ref[0], wo_ref[0])
    for hh in range(1, o_ref.shape[0]):
        acc = acc + _dot(o_ref[hh], wo_ref[hh])
    y_ref[...] = x_ref[...] + m_ref[0][2:3] * acc


def _attn_out(o, x2, mod, w_o, tiles_per_sample):
    n, d = x2.shape
    nq = o.shape[0]
    row_spec = pl.BlockSpec((TM, d), lambda i: (i, 0))
    return pl.pallas_call(
        _attn_out_kernel,
        out_shape=jax.ShapeDtypeStruct((n, d), F32),
        grid=(n // TM,),
        in_specs=[
            pl.BlockSpec((nq, TM, HEAD_DIM), lambda i: (0, i, 0)),
            row_spec,
            pl.BlockSpec((1, N_MOD, d), lambda i: (_mod_row(i, tiles_per_sample), 0, 0)),
            _const_spec((nq, HEAD_DIM, d)),
        ],
        out_specs=row_spec,
        compiler_params=_cparams("arbitrary"),
        name="gqa_out_proj",
    )(o, x2, mod, w_o)


def _router_kernel(x_ref, m_ref, gain_ref, wr_ref, f_o, afft_o):
    m = m_ref[0]
    f = _norm_mod(x_ref[...], gain_ref[...], m[3:4], m[4:5])
    f_o[...] = f
    logits = _mm(f, wr_ref[...], 3)
    lane = lax.broadcasted_iota(jnp.int32, logits.shape, 1)
    logits = jnp.where(lane < N_EXPERTS, logits, -1e30)
    e = jnp.exp(logits - jnp.max(logits, axis=-1, keepdims=True))
    aff = e / jnp.sum(e, axis=-1, keepdims=True)
    afft_o[0] = aff.T[:N_EXPERTS]


def _moe_router(x2, mod, gain, w_router, batch, tiles_per_sample):
    n, d = x2.shape
    tt = n // batch
    return pl.pallas_call(
        _router_kernel,
        out_shape=(jax.ShapeDtypeStruct((n, d), F32), jax.ShapeDtypeStruct((batch, N_EXPERTS, tt), F32)),
        grid=(n // TM,),
        in_specs=[
            pl.BlockSpec((TM, d), lambda i: (i, 0)),
            pl.BlockSpec((1, N_MOD, d), lambda i: (_mod_row(i, tiles_per_sample), 0, 0)),
            _const_spec((1, d)),
            _const_spec((d, LANES)),
        ],
        out_specs=(pl.BlockSpec((TM, d), lambda i: (i, 0)),
                   pl.BlockSpec((1, N_EXPERTS, TM), lambda i: (i // tiles_per_sample, 0, i % tiles_per_sample))),
        compiler_params=_cparams("arbitrary"),
        name="moe_router",
    )(x2, mod, gain, w_router)


def _prefix_excl(mask, tri_excl):
    xb = mask.astype(BF16)
    carry = jnp.zeros((mask.shape[0], 1), F32)
    outs = []
    for blk in range(mask.shape[1] // LANES):
        piece = xb[:, blk * LANES:(blk + 1) * LANES]
        outs.append(_dot(piece, tri_excl) + carry)
        carry = carry + jnp.sum(piece.astype(F32), axis=1, keepdims=True)
    return jnp.concatenate(outs, axis=1) if len(outs) > 1 else outs[0]


def _top_cap(a, cap, tri_excl):
    ai = lax.bitcast_convert_type(a, jnp.int32)

    def body(i, thr):
        cand = thr | jnp.left_shift(jnp.int32(1), 30 - i)
        cnt = jnp.sum((ai >= cand).astype(F32), axis=1, keepdims=True)
        return jnp.where(cnt >= cap, cand, thr)

    thr = lax.fori_loop(0, 31, body, jnp.zeros((a.shape[0], 1), jnp.int32))
    gt = ai > thr
    eq = ai == thr
    need = cap - jnp.sum(gt.astype(F32), axis=1, keepdims=True)
    sel = jnp.logical_or(gt, jnp.logical_and(eq, _prefix_excl(eq, tri_excl) < need))
    return sel, _prefix_excl(sel, tri_excl)


def _select_kernel(afft_ref, tri_ref, idx_o, gate_o, sel_s, pos_s, *, sets):
    e = pl.program_id(1)

    @pl.when(e == 0)
    def _():
        for off, n, cap, _ in sets:
            sel, pos = _top_cap(afft_ref[0, :, off:off + n], cap, tri_ref[...])
            sel_s[:, off:off + n] = sel.astype(F32)
            pos_s[:, off:off + n] = pos

    for off, n, cap, slot0 in sets:
        a_e = afft_ref[0, pl.ds(e, 1), off:off + n]
        sel_e = sel_s[pl.ds(e, 1), off:off + n]
        pos_e = pos_s[pl.ds(e, 1), off:off + n]
        slot = lax.broadcasted_iota(jnp.int32, (cap, n), 0).astype(F32)
        onehot = jnp.where(jnp.logical_and(pos_e == slot, sel_e > 0), 1.0, 0.0).astype(BF16)
        tok = lax.broadcasted_iota(jnp.int32, (16, n), 1)
        rid = lax.broadcasted_iota(jnp.int32, (16, n), 0)
        a_h, a_m, a_l = _split3(a_e)
        rows = jnp.where(rid == 0, jnp.right_shift(tok, 6).astype(F32),
               jnp.where(rid == 1, jnp.bitwise_and(tok, 63).astype(F32),
               jnp.where(rid == 2, a_h.astype(F32),
               jnp.where(rid == 3, a_m.astype(F32),
               jnp.where(rid == 4, a_l.astype(F32), 0.0)))))
        res = _dot_nt(rows.astype(BF16), onehot)
        idx_o[0, 0, :, slot0:slot0 + cap] = (res[0:1] * 64.0 + res[1:2]).astype(jnp.int32) + off
        gate_o[0, 0, :, slot0:slot0 + cap] = res[2:3] + (res[3:4] + res[4:5])


def _moe_select(afft, sets, n_slots):
    batch, ne, tt = afft.shape
    tri = (jnp.arange(LANES)[:, None] < jnp.arange(LANES)[None, :]).astype(BF16)
    slot_spec = pl.BlockSpec((1, 1, 1, n_slots), lambda bb, e: (bb, e, 0, 0))
    return pl.pallas_call(
        functools.partial(_select_kernel, sets=sets),
        out_shape=(jax.ShapeDtypeStruct((batch, ne, 1, n_slots), jnp.int32),
                   jax.ShapeDtypeStruct((batch, ne, 1, n_slots), F32)),
        grid=(batch, ne),
        in_specs=[pl.BlockSpec((1, ne, tt), lambda bb, e: (bb, 0, 0)), _const_spec((LANES, LANES))],
        out_specs=(slot_spec, slot_spec),
        scratch_shapes=[pltpu.VMEM((ne, tt), F32), pltpu.VMEM((ne, tt), F32)],
        compiler_params=_cparams("arbitrary", "arbitrary"),
        name="moe_select",
    )(afft, tri)


def _gather_kernel(idx_ref, f_ref, xe_o, buf, *, n_slots):
    def body(c, _):
        r = idx_ref[0, 0, 0, c]
        buf[pl.ds(c, 1), :] = f_ref[pl.ds(r, 1), :]
        return 0

    lax.fori_loop(0, n_slots, body, 0, unroll=8)
    xe_o[0, 0] = buf[...].astype(BF16)


def _moe_gather(idx, f, batch):
    n, d = f.shape
    tt = n // batch
    ne, n_slots = idx.shape[1], idx.shape[3]
    return pl.pallas_call(
        functools.partial(_gather_kernel, n_slots=n_slots),
        out_shape=jax.ShapeDtypeStruct((batch, ne, n_slots, d), BF16),
        grid=(batch, ne),
        in_specs=[
            pl.BlockSpec((1, 1, 1, n_slots), lambda bb, e: (bb, e, 0, 0), memory_space=pltpu.SMEM),
            pl.BlockSpec((tt, d), lambda bb, e: (bb, 0)),
        ],
        out_specs=pl.BlockSpec((1, 1, n_slots, d), lambda bb, e: (bb, e, 0, 0)),
        scratch_shapes=[pltpu.VMEM((n_slots, d), F32)],
        compiler_params=_cparams("arbitrary", "arbitrary"),
        name="moe_gather",
    )(idx, f)


def _expert_kernel(xe_ref, wg_ref, wu_ref, wd_ref, ye_o):
    fi = pl.program_id(2)
    bh, _, s, d = xe_ref.shape
    x = xe_ref[...].reshape(bh * s, d)
    gp = _dot(x, wg_ref[0, 0].astype(BF16))
    up = _dot(x, wu_ref[0, 0].astype(BF16))
    h = (_silu(gp) * up).astype(BF16)
    y = _dot(h, wd_ref[0, 0].astype(BF16)).reshape(bh, 1, s, d)

    @pl.when(fi == 0)
    def _():
        ye_o[...] = y

    @pl.when(fi > 0)
    def _():
        ye_o[...] += y


def _moe_experts(xe, w_gate_up, w_down, layer, halves=2, tf=256):
    batch, ne, s, d = xe.shape
    de = w_down.shape[2]
    nf = de // tf
    bh = batch // halves
    tok = pl.BlockSpec((bh, 1, s, d), lambda e, mh, f: (mh, e, 0, 0))
    return pl.pallas_call(
        _expert_kernel,
        out_shape=jax.ShapeDtypeStruct((batch, ne, s, d), F32),
        grid=(ne, halves, nf),
        in_specs=[
            tok,
            pl.BlockSpec((1, 1, d, tf), lambda e, mh, f: (layer, e, 0, f)),
            pl.BlockSpec((1, 1, d, tf), lambda e, mh, f: (layer, e, 0, nf + f)),
            pl.BlockSpec((1, 1, tf, d), lambda e, mh, f: (layer, e, f, 0)),
        ],
        out_specs=tok,
        compiler_params=_cparams("arbitrary", "arbitrary", "arbitrary"),
        name="moe_experts",
    )(xe, w_gate_up, w_gate_up, w_down)


def _combine_kernel(idx_ref, gate_ref, ye_ref, out_o, *, n_slots):
    e = pl.program_id(1)

    @pl.when(e == 0)
    def _():
        out_o[...] = jnp.zeros_like(out_o)

    def body(c, _):
        r = idx_ref[0, 0, 0, c]
        out_o[pl.ds(r, 1), :] += gate_ref[0, 0, 0, c] * ye_ref[0, 0, pl.ds(c, 1), :]
        return 0

    lax.fori_loop(0, n_slots, body, 0, unroll=8)


def _moe_combine(idx, gate, ye, tt):
    batch, ne, n_slots, d = ye.shape
    smem = pl.BlockSpec((1, 1, 1, n_slots), lambda bb, e: (bb, e, 0, 0), memory_space=pltpu.SMEM)
    return pl.pallas_call(
        functools.partial(_combine_kernel, n_slots=n_slots),
        out_shape=jax.ShapeDtypeStruct((batch * tt, d), F32),
        grid=(batch, ne),
        in_specs=[smem, smem, pl.BlockSpec((1, 1, n_slots, d), lambda bb, e: (bb, e, 0, 0))],
        out_specs=pl.BlockSpec((tt, d), lambda bb, e: (bb, 0)),
        compiler_params=_cparams("arbitrary", "arbitrary"),
        name="moe_combine",
    )(idx, gate, ye)


def _residual_kernel(x_ref, y_ref, m_ref, o_ref):
    o_ref[...] = x_ref[...] + m_ref[0][5:6] * y_ref[...]


def _moe_residual(x2, y, mod, tiles_per_sample):
    n, d = x2.shape
    row_spec = pl.BlockSpec((TM, d), lambda i: (i, 0))
    return pl.pallas_call(
        _residual_kernel,
        out_shape=jax.ShapeDtypeStruct((n, d), F32),
        grid=(n // TM,),
        in_specs=[row_spec, row_spec,
                  pl.BlockSpec((1, N_MOD, d), lambda i: (_mod_row(i, tiles_per_sample), 0, 0))],
        out_specs=row_spec,
        compiler_params=_cparams("arbitrary"),
        name="moe_residual",
    )(x2, y, mod)


def _final_kernel(x_ref, g_ref, o_ref):
    x = x_ref[...]
    ms = jnp.mean(x * x, axis=-1, keepdims=True)
    o_ref[0] = (x * lax.rsqrt(ms + NORM_EPS)) * g_ref[...]


def _final_norm(x2, gain, batch, tiles_per_sample):
    n, d = x2.shape
    lat_tiles = tiles_per_sample - 1
    return pl.pallas_call(
        _final_kernel,
        out_shape=jax.ShapeDtypeStruct((batch, lat_tiles * TM, d), F32),
        grid=(batch, lat_tiles),
        in_specs=[pl.BlockSpec((TM, d), lambda bb, t: (bb * tiles_per_sample + 1 + t, 0)), _const_spec((1, d))],
        out_specs=pl.BlockSpec((1, TM, d), lambda bb, t: (bb, t, 0)),
        compiler_params=_cparams("arbitrary", "arbitrary"),
        name="final_norm",
    )(x2, gain)


def _segment_matrices(width):
    heads = jnp.arange(width) // HEAD_DIM
    seg = (heads[:, None] == jnp.arange(LANES)[None, :]).astype(BF16)
    return seg, seg.T


def _rope_tables(seq, ctx_len):
    t = jnp.arange(seq)
    pos = jnp.stack([(t // GRID_W).astype(F32), (t % GRID_W).astype(F32)], axis=-1)
    half = HEAD_DIM // 4
    inv_freq = ROPE_THETA ** (-jnp.arange(0, 2 * half, 2, dtype=F32) / (2 * half))
    ang = pos[:, :, None] * inv_freq
    cos, sin = jnp.cos(ang), jnp.sin(ang)
    zero = jnp.zeros_like(sin)
    per_head = lambda first, second: jnp.concatenate([first, second], axis=-1).reshape(seq, HEAD_DIM)
    cos_h = per_head(cos, cos)
    sin_lo = per_head(-sin, zero)
    sin_hi = per_head(zero, sin)
    pad = lambda tab, fill: jnp.concatenate([jnp.full((ctx_len, HEAD_DIM), fill, F32), tab], axis=0)
    two = lambda tab: jnp.concatenate([tab, tab], axis=1)
    return two(pad(cos_h, 1.0)), two(pad(sin_lo, 0.0)), two(pad(sin_hi, 0.0))


def kernel(x, c, ctx, c_ctx, mod_w, mod_b, norm_mix, norm_ffn, rwkv_mu, rwkv_w_rkv, rwkv_w0, rwkv_w1, rwkv_w2,
           rwkv_a0, rwkv_a1, rwkv_a2, rwkv_g1, rwkv_g2, rwkv_k_k, rwkv_k_a, rwkv_r_k, rwkv_ln_w, rwkv_ln_b,
           rwkv_w_o, attn_w_qkv, attn_q_gain, attn_k_gain, attn_w_o, moe_router, moe_w_gate_up, moe_w_down,
           final_norm):
    batch, seq, d = x.shape
    ctx_len = ctx.shape[1]
    depth = mod_w.shape[0]
    tt = ctx_len + seq
    tps = tt // TM
    assert ctx_len == TM and seq % TM == 0 and CHUNK == HEAD_DIM and batch + 1 <= 16
    nq = d // HEAD_DIM

    x2 = jnp.concatenate([ctx, x], axis=1).reshape(batch * tt, d)
    cc = jnp.zeros((16, d), F32).at[0].set(c_ctx).at[1:batch + 1].set(c)
    mods = _mod_vectors(cc, mod_w, mod_b).reshape(depth, 16, N_MOD, d)
    seg, segt = _segment_matrices(d)
    segk, segtk = _segment_matrices(ATTN_KV_HEADS * HEAD_DIM)
    rope = _rope_tables(seq, ctx_len)
    cap_l = EC_CAPACITY * seq // N_EXPERTS
    cap_c = EC_CAPACITY * ctx_len // N_EXPERTS
    sets = ((ctx_len, seq, cap_l, 0), (0, ctx_len, cap_c, cap_l))
    router_pad = jnp.zeros((depth, d, LANES), F32).at[:, :, :N_EXPERTS].set(moe_router)

    ia = ib = 0
    for i in range(depth):
        mod = mods[i]
        if i % 2 == 0:
            zpad = jnp.zeros((HEAD_DIM, d), F32)
            lora_pad = lambda w: jnp.stack([jnp.concatenate([w[0], zpad], 0), jnp.concatenate([zpad, w[1]], 0)]).astype(BF16)
            p = {
                "mu": rwkv_mu[ia], "w_rkv": rwkv_w_rkv[ia].astype(BF16),
                "w1": jnp.concatenate([rwkv_w1[ia, 0], rwkv_w1[ia, 1]], axis=1).astype(BF16), "w2": lora_pad(rwkv_w2[ia]),
                "w0": rwkv_w0[ia],
                "a1": jnp.concatenate([rwkv_a1[ia, 0], rwkv_a1[ia, 1]], axis=1).astype(BF16), "a2": lora_pad(rwkv_a2[ia]),
                "a0": rwkv_a0[ia],
                "g1": rwkv_g1[ia].astype(BF16), "g2": rwkv_g2[ia].astype(BF16),
                "k_k": rwkv_k_k[ia].reshape(1, d), "k_a": rwkv_k_a[ia].reshape(1, d),
                "r_k": rwkv_r_k[ia].reshape(1, d), "ln_w": rwkv_ln_w[ia].reshape(1, d), "ln_b": rwkv_ln_b[ia].reshape(1, d),
                "w_o": rwkv_w_o[ia].astype(BF16),
            }
            r, v, kk, g, lw, kd, b = _rwkv_inputs(x2, mod, norm_mix[i].reshape(1, d), p, seg, segt, tps)
            y = _rwkv_scan(r, v, kk, lw, kd, b, batch, ctx_len)
            x2 = _rwkv_readout(y, r, kd, v, g, x2, mod, p, seg, segt, tps)
            ia += 1
        else:
            p = {
                "w_qkv": attn_w_qkv[ib].astype(BF16),
                "q_gain": jnp.tile(attn_q_gain[ib], nq).reshape(1, d),
                "k_gain": jnp.tile(attn_k_gain[ib], ATTN_KV_HEADS).reshape(1, ATTN_KV_HEADS * HEAD_DIM),
            }
            q, k, v = _gqa_project(x2, mod, norm_mix[i].reshape(1, d), p, rope, seg, segt, segk, segtk, tps)
            o = _attention(q, k, v, batch, tps, tps - 1, 1, tt)
            o = _attention(q, k, v, batch, tps, 1, 0, ctx_len, into=o)
            x2 = _attn_out(o, x2, mod, attn_w_o[ib].astype(BF16).reshape(nq, HEAD_DIM, d), tps)
            ib += 1
        f, afft = _moe_router(x2, mod, norm_ffn[i].reshape(1, d), router_pad[i], batch, tps)
        idx, gate = _moe_select(afft, sets, cap_l + cap_c)
        xe = _moe_gather(idx, f, batch)
        ye = _moe_experts(xe, moe_w_gate_up, moe_w_down, i)
        x2 = _moe_residual(x2, _moe_combine(idx, gate, ye, tt), mod, tps)
    return _final_norm(x2, final_norm.reshape(1, d), batch, tps)
```

```python
import functools

import jax
import jax.numpy as jnp
from jax import lax
from jax.experimental import pallas as pl
from jax.experimental.pallas import tpu as pltpu

F32 = jnp.float32
BF16 = jnp.bfloat16

HEAD_DIM = 64
N_MOD = 6
NORM_EPS = 1e-6
GN_EPS = 64e-5
ROPE_THETA = 10000.0
GRID_W = 64
ATTN_KV_HEADS = 4
N_EXPERTS = 16
EC_CAPACITY = 2
TM = 256
CHUNK = 64
LANES = 128
VMEM_LIMIT = 56 * 1024 * 1024
NEG_EXP_M05 = -0.6065306597126334


def _cparams(*sem):
    return pltpu.CompilerParams(dimension_semantics=sem, vmem_limit_bytes=VMEM_LIMIT)


def _split2(x):
    hi = x.astype(BF16)
    lo = (x - hi.astype(F32)).astype(BF16)
    return hi, lo


def _split3(x):
    hi = x.astype(BF16)
    r1 = x - hi.astype(F32)
    mid = r1.astype(BF16)
    lo = (r1 - mid.astype(F32)).astype(BF16)
    return hi, mid, lo


def _dot(a, b):
    return jnp.dot(a, b, preferred_element_type=F32)


def _dot_nt(a, b):
    return lax.dot_general(a, b, (((1,), (1,)), ((), ())), preferred_element_type=F32)


def _dot_tn(a, b):
    return lax.dot_general(a, b, (((0,), (0,)), ((), ())), preferred_element_type=F32)


def _bdot(a, b):
    return _dot(a.astype(BF16), b.astype(BF16))


def _mm(a, b, passes, kind="nn"):
    f = {"nn": _dot, "nt": _dot_nt, "tn": _dot_tn}[kind]
    if passes == 1:
        return f(a.astype(BF16), b.astype(BF16))
    ah, al = _split2(a)
    bh, bl = _split2(b)
    return f(ah, bh) + (f(ah, bl) + f(al, bh))


def _dot_exact_rhs(a, b_bf16):
    h, m, l = _split3(a)
    return _dot(h, b_bf16) + (_dot(m, b_bf16) + _dot(l, b_bf16))


def _seg_sum(x, seg, segt):
    s = _dot_exact_rhs(x, seg)
    return _dot_exact_rhs(s, segt)


def _norm_mod(x, gain, shift, scale):
    ms = jnp.mean(x * x, axis=-1, keepdims=True)
    y = x * lax.rsqrt(ms + NORM_EPS)
    return (y * gain) * (1.0 + scale) + shift


def _sigmoid(x):
    return 1.0 / (1.0 + jnp.exp(-x))


def _silu(x):
    return x * _sigmoid(x)


def _mod_row(i, tiles_per_sample):
    return jnp.where(i % tiles_per_sample == 0, 0, 1 + i // tiles_per_sample)


def _const_spec(shape):
    nd = len(shape)
    return pl.BlockSpec(shape, lambda *_: (0,) * nd)


def _mod_kernel(c_ref, w_ref, b_ref, o_ref):
    s = _silu(c_ref[...])
    o_ref[0] = _mm(s, w_ref[0], 3) + b_ref[0]


def _mod_vectors(cc, mod_w, mod_b):
    depth, d, n = mod_w.shape
    tn = 1536
    return pl.pallas_call(
        _mod_kernel,
        out_shape=jax.ShapeDtypeStruct((depth, 16, n), F32),
        grid=(depth, n // tn),
        in_specs=[
            pl.BlockSpec((16, d), lambda l, j: (0, 0)),
            pl.BlockSpec((1, d, tn), lambda l, j: (l, 0, j)),
            pl.BlockSpec((1, 1, tn), lambda l, j: (l, 0, j)),
        ],
        out_specs=pl.BlockSpec((1, 16, tn), lambda l, j: (l, 0, j)),
        compiler_params=_cparams("arbitrary", "arbitrary"),
        name="mod_vectors",
    )(cc, mod_w, mod_b.reshape(depth, 1, n))


def _rwkv_in_kernel(x_ref, xp_ref, xn_ref, m_ref, gain_ref, mu_ref, wrkv_ref, w1_ref, w2_ref, w0_ref,
                    a1_ref, a2_ref, a0_ref, g1_ref, g2_ref, kk_ref_, ka_ref, seg_ref, segt_ref,
                    r_o, v_o, kk_o, g_o, lw_o, kd_o, b_o, *, tiles_per_sample):
    i = pl.program_id(0)
    j = i % tiles_per_sample
    m = m_ref[0]
    shift, scale = m[0:1], m[1:2]
    gain = gain_ref[...]
    h = _norm_mod(x_ref[...], gain, shift, scale)
    has_prev = (j >= 2).astype(F32)
    has_next = jnp.logical_and(j >= 1, j <= tiles_per_sample - 2).astype(F32)
    hp_row = _norm_mod(xp_ref[7:8, :], gain, shift, scale) * has_prev
    hn_row = _norm_mod(xn_ref[0:1, :], gain, shift, scale) * has_next
    row = lax.broadcasted_iota(jnp.int32, h.shape, 0)
    h_prev = jnp.where(row == 0, hp_row, pltpu.roll(h, 1, 0))
    h_next = jnp.where(row == TM - 1, hn_row, pltpu.roll(h, TM - 1, 0))
    xx = 0.5 * (h_prev + h_next) - h
    mu = mu_ref[...]

    def mix(n):
        return (h + xx * mu[n:n + 1]).astype(BF16)

    r = _dot(mix(0), wrkv_ref[0])
    k = _dot(mix(1), wrkv_ref[1])
    v = _dot(mix(2), wrkv_ref[2])
    tw = jnp.tanh(_dot(mix(3), w1_ref[...])).astype(BF16)
    ua = _dot(mix(4), a1_ref[...]).astype(BF16)
    g = _dot(_sigmoid(_dot(mix(5), g1_ref[...])).astype(BF16), g2_ref[...])
    kk = k * kk_ref_[...]
    n2 = _seg_sum(kk * kk, seg_ref[...], segt_ref[...])
    kk = kk / jnp.maximum(jnp.sqrt(n2), 1e-12)
    r_o[...] = r
    v_o[...] = v
    kk_o[...] = kk
    g_o[...] = g
    ka = ka_ref[...]
    for z in range(2):
        w_pre = w0_ref[z:z + 1, :] + _dot(tw, w2_ref[z])
        lw_o[z] = NEG_EXP_M05 * _sigmoid(w_pre)
        a =_sigmoid(a0_ref[z:z + 1, :] + _dot(ua, a2_ref[z]))
        kd_o[z] = k * (1.0 + (a - 1.0) * ka)
        b_o[z] = kk * a


def _rwkv_inputs(x2, mod, gain, p, seg, segt, tiles_per_sample):
    n, d = x2.shape
    nt = n // TM
    blk8 = TM // 8
    last8 = n // 8 - 1
    row_spec = pl.BlockSpec((TM, d), lambda i: (i, 0))
    dir_spec = pl.BlockSpec((2, TM, d), lambda i: (0, i, 0))
    tok = jax.ShapeDtypeStruct((n, d), F32)
    tok2 = jax.ShapeDtypeStruct((2, n, d), F32)
    return pl.pallas_call(
        functools.partial(_rwkv_in_kernel, tiles_per_sample=tiles_per_sample),
        out_shape=(tok, tok, tok, tok, tok2, tok2, tok2),
        grid=(nt,),
        in_specs=[
            row_spec,
            pl.BlockSpec((8, d), lambda i: (jnp.maximum(i * blk8 - 1, 0), 0)),
            pl.BlockSpec((8, d), lambda i: (jnp.minimum((i + 1) * blk8, last8), 0)),
            pl.BlockSpec((1, N_MOD, d), lambda i: (_mod_row(i, tiles_per_sample), 0, 0)),
            _const_spec((1, d)),
            _const_spec((6, d)),
            _const_spec((3, d, d)),
            _const_spec((d, LANES)),
            _const_spec((2, LANES, d)),
            _const_spec((2, d)),
            _const_spec((d, LANES)),
            _const_spec((2, LANES, d)),
            _const_spec((2, d)),
            _const_spec((d, LANES)),
            _const_spec((LANES, d)),
            _const_spec((1, d)),
            _const_spec((1, d)),
            _const_spec((d, LANES)),
            _const_spec((LANES, d)),
        ],
        out_specs=(row_spec, row_spec, row_spec, row_spec, dir_spec, dir_spec, dir_spec),
        compiler_params=_cparams("arbitrary"),
        name="rwkv_inputs",
    )(x2, x2, x2, mod, gain, p["mu"], p["w_rkv"], p["w1"], p["w2"], p["w0"], p["a1"], p["a2"], p["a0"],
      p["g1"], p["g2"], p["k_k"], p["k_a"], seg, segt)


def _scan_consts(reverse):
    sgn = -1 if reverse else 1
    n2 = 2 * CHUNK
    row = lax.broadcasted_iota(jnp.int32, (n2, n2), 0)
    col = lax.broadcasted_iota(jnp.int32, (n2, n2), 1)
    same = (row // CHUNK) == (col // CHUNK)
    dt = (row % CHUNK - col % CHUNK) * sgn
    rc = lax.broadcasted_iota(jnp.int32, (CHUNK, CHUNK), 0)
    cc = lax.broadcasted_iota(jnp.int32, (CHUNK, CHUNK), 1)
    return {
        "same": same,
        "strict": jnp.logical_and(same, dt > 0),
        "incl": jnp.logical_and(same, dt >= 0),
        "eye": (row == col).astype(F32),
        "tri": ((rc - cc) * sgn >= 0).astype(BF16),
        "head0": lax.broadcasted_iota(jnp.int32, (CHUNK, LANES), 1) < HEAD_DIM,
    }


def _scan_units(units, passes):
    n2 = 2 * CHUNK
    every = lambda f: [f(u) for u in units]

    def stack(u, x):
        return jnp.concatenate([jnp.where(u["c"]["head0"], x, 0.0), jnp.where(u["c"]["head0"], 0.0, x)], axis=0)

    def prep(u):
        cl = _dot_exact_rhs_left(u["c"]["tri"], u["lw"])
        tot = jnp.sum(u["lw"], axis=0, keepdims=True)
        e_ncl = jnp.exp(-cl)
        e_end = jnp.exp(tot - cl)
        u["q2"] = jnp.concatenate([stack(u, u["kk"] * jnp.exp(cl - u["lw"])), stack(u, u["r"] * jnp.exp(cl))], axis=0)
        u["k2"] = jnp.concatenate([stack(u, u["kd"] * e_ncl), stack(u, u["b"] * e_ncl)], axis=0)
        u["ket"] = jnp.concatenate([u["kd"] * e_end, -(u["b"] * e_end)], axis=0).T
        u["g_col"] = jnp.sum(jnp.where(u["c"]["eye"] > 0, jnp.exp(tot), 0.0), axis=1, keepdims=True)
        u["vs"] = stack(u, u["v"])

    every(prep)
    a_all = every(lambda u: _mm(u["q2"], u["k2"], passes["a"], "nt"))
    qm = every(lambda u: _mm(u["q2"], u["m0"], passes["qm"]))
    for u, a in zip(units, a_all):
        c = u["c"]
        u["l_kk"] = jnp.where(c["strict"], a[:n2, :n2], 0.0)
        u["l_rk"] = jnp.where(c["incl"], a[n2:, :n2], 0.0)
        u["l_rb"] = jnp.where(c["incl"], a[n2:, n2:], 0.0)
        u["pw"] = -jnp.where(c["strict"], a[:n2, n2:], 0.0)
        u["inv"] = c["eye"] + u["pw"]
    lv = every(lambda u: _mm(u["l_kk"], u["vs"], passes["lv"]))
    for _ in range(CHUNK.bit_length() - 2):
        pw = every(lambda u: _mm(u["pw"], u["pw"], passes["inv"]))
        for u, x in zip(units, pw):
            u["pw"] = x
        upd = every(lambda u: _mm(u["inv"], u["pw"], passes["inv"]))
        for u, x in zip(units, upd):
            u["inv"] = u["inv"] + x
    us = [_mm(u["inv"], q[:n2] + t, passes["us"]) for u, q, t in zip(units, qm, lv)]
    ys = [q[n2:] + _mm(jnp.concatenate([u["l_rk"], -u["l_rb"]], axis=1), jnp.concatenate([u["vs"], s], axis=0), passes["ys"])
          for u, q, s in zip(units, qm, us)]
    new = [_mm(u["ket"], jnp.concatenate([u["v"], s[:CHUNK] + s[CHUNK:]], axis=0), passes["upd"]) for u, s in zip(units, us)]
    ms = [u["m0"] * u["g_col"] + jnp.where(u["c"]["same"], x, 0.0) for u, x in zip(units, new)]
    return [y[:CHUNK] + y[CHUNK:] for y in ys], ms


def _dot_exact_rhs_left(a_bf16, b):
    h, m, l = _split3(b)
    return _dot(a_bf16, h) + (_dot(a_bf16, m) + _dot(a_bf16, l))


SCAN_PASSES = {"a": 1, "qm": 1, "lv": 1, "inv": 1, "us": 1, "ys": 1, "upd": 1}


def _scan_kernel(rf_ref, vf_ref, kkf_ref, rr_ref, vr_ref, kkr_ref, lwf_ref, kdf_ref, bf_ref, lwr_ref, kdr_ref,
                 br_ref, yf_ref, yr_ref, m_ref, *, pairs):
    @pl.when(pl.program_id(2) == 0)
    def _():
        m_ref[...] = jnp.zeros_like(m_ref)

    dirs = (
        (_scan_consts(False), rf_ref, vf_ref, kkf_ref, lwf_ref, kdf_ref, bf_ref, yf_ref),
        (_scan_consts(True), rr_ref, vr_ref, kkr_ref, lwr_ref, kdr_ref, br_ref, yr_ref),
    )
    units = []
    for z, (consts, r_ref, v_ref, kk_ref, lw_ref, kd_ref, b_ref, _) in enumerate(dirs):
        for p in range(pairs):
            sl = slice(p * LANES, (p + 1) * LANES)
            units.append({"c": consts, "r": r_ref[:, sl], "v": v_ref[:, sl], "kk": kk_ref[:, sl], "lw": lw_ref[0, :, sl],
                          "kd": kd_ref[0, :, sl], "b": b_ref[0, :, sl], "m0": m_ref[z * pairs + p]})
    ys, ms = _scan_units(units, SCAN_PASSES)
    for i, (y, m1) in enumerate(zip(ys, ms)):
        z, p = divmod(i, pairs)
        dirs[z][-1][0, :, p * LANES:(p + 1) * LANES] = y
        m_ref[i] = m1


def _rwkv_scan(r, v, kk, lw, kd, b, batch, ctx_len, pairs=8):
    n, d = r.shape
    tt = n // batch
    nch = tt // CHUNK
    nch_ctx = ctx_len // CHUNK
    width = pairs * LANES
    groups = d // width

    fwd_row = lambda bb, c: bb * nch + c
    rev_row = lambda bb, c: bb * nch + jnp.where(c < nch_ctx, nch_ctx - 1 - c, nch + nch_ctx - 1 - c)
    shared = lambda row: pl.BlockSpec((CHUNK, width), lambda bb, g, c: (row(bb, c), g))
    per_dir = lambda z, row: pl.BlockSpec((1, CHUNK, width), lambda bb, g, c: (z, row(bb, c), g))
    return pl.pallas_call(
        functools.partial(_scan_kernel, pairs=pairs),
        out_shape=(jax.ShapeDtypeStruct((1, n, d), F32), jax.ShapeDtypeStruct((1, n, d), F32)),
        grid=(batch, groups, nch),
        in_specs=[shared(fwd_row)] * 3 + [shared(rev_row)] * 3 + [per_dir(0, fwd_row)] * 3 + [per_dir(1, rev_row)] * 3,
        out_specs=(per_dir(0, fwd_row), per_dir(0, rev_row)),
        scratch_shapes=[pltpu.VMEM((2 * pairs, LANES, LANES), F32)],
        compiler_params=_cparams("arbitrary", "arbitrary", "arbitrary"),
        name="rwkv_scan",
    )(r, v, kk, r, v, kk, lw, kd, b, lw, kd, b)


def _rwkv_out_kernel(y0_ref, y1_ref, r_ref, kd0_ref, kd1_ref, v_ref, g_ref, x_ref, m_ref, rk_ref, lnw_ref,
                     lnb_ref, wo_ref, seg_ref, segt_ref, o_ref):
    seg, segt = seg_ref[...], segt_ref[...]
    y = y0_ref[0] + y1_ref[0]
    mean = _seg_sum(y, seg, segt) * (1.0 / HEAD_DIM)
    dy = y - mean
    var = _seg_sum(dy * dy, seg, segt) * (1.0 / HEAD_DIM)
    yn = (dy * lax.rsqrt(var + GN_EPS)) * lnw_ref[...] + lnb_ref[...]
    bonus = _seg_sum(r_ref[...] * (kd0_ref[0] + kd1_ref[0]) * rk_ref[...], seg, segt)
    out = (yn + bonus * v_ref[...]) * g_ref[...]
    o = _dot(out.astype(BF16), wo_ref[...])
    o_ref[...] = x_ref[...] + m_ref[0][2:3] * o


def _rwkv_readout(yf, yr, r, kd, v, g, x2, mod, p, seg, segt, tiles_per_sample):
    n, d = x2.shape
    row_spec = pl.BlockSpec((TM, d), lambda i: (i, 0))
    return pl.pallas_call(
        _rwkv_out_kernel,
        out_shape=jax.ShapeDtypeStruct((n, d), F32),
        grid=(n // TM,),
        in_specs=[
            pl.BlockSpec((1, TM, d), lambda i: (0, i, 0)),
            pl.BlockSpec((1, TM, d), lambda i: (0, i, 0)),
            row_spec,
            pl.BlockSpec((1, TM, d), lambda i: (0, i, 0)),
            pl.BlockSpec((1, TM, d), lambda i: (1, i, 0)),
            row_spec, row_spec, row_spec,
            pl.BlockSpec((1, N_MOD, d), lambda i: (_mod_row(i, tiles_per_sample), 0, 0)),
            _const_spec((1, d)), _const_spec((1, d)), _const_spec((1, d)),
            _const_spec((d, d)),
            _const_spec((d, LANES)), _const_spec((LANES, d)),
        ],
        out_specs=row_spec,
        compiler_params=_cparams("arbitrary"),
        name="rwkv_readout",
    )(yf, yr, r, kd, kd, v, g, x2, mod, p["r_k"], p["ln_w"], p["ln_b"], p["w_o"], seg, segt)


def _rope(x, cos, sin_lo, sin_hi):
    w = x.shape[1]
    reps = w // LANES
    tile = lambda t: jnp.concatenate([t] * reps, axis=1) if reps > 1 else t
    half = HEAD_DIM // 4
    return x * tile(cos) + pltpu.roll(x, w - half, 1) * tile(sin_lo) + pltpu.roll(x, half, 1) * tile(sin_hi)


def _gqa_proj_kernel(x_ref, m_ref, gain_ref, w_ref, qg_ref, kg_ref, cos_ref, slo_ref, shi_ref, seg_ref, segt_ref,
                     segk_ref, segtk_ref, q_o, k_o, v_o, *, d, dkv):
    m = m_ref[0]
    h = _norm_mod(x_ref[...], gain_ref[...], m[0:1], m[1:2]).astype(BF16)
    qkv = _dot(h, w_ref[...])
    q, k, v = qkv[:, :d], qkv[:, d:d + dkv], qkv[:, d + dkv:]
    cos, slo, shi = cos_ref[...], slo_ref[...], shi_ref[...]
    qms = _seg_sum(q * q, seg_ref[...], segt_ref[...]) * (1.0 / HEAD_DIM)
    q = (q * lax.rsqrt(qms + NORM_EPS)) * qg_ref[...]
    kms = _seg_sum(k * k, segk_ref[...], segtk_ref[...]) * (1.0 / HEAD_DIM)
    k = (k * lax.rsqrt(kms + NORM_EPS)) * kg_ref[...]
    q = (_rope(q, cos, slo, shi) * (HEAD_DIM ** -0.5)).astype(BF16)
    k = _rope(k, cos, slo, shi).astype(BF16)
    v = v.astype(BF16)
    for hh in range(d // HEAD_DIM):
        q_o[hh] = q[:, hh * HEAD_DIM:(hh + 1) * HEAD_DIM]
    for hh in range(dkv // HEAD_DIM):
        k_o[hh] = k[:, hh * HEAD_DIM:(hh + 1) * HEAD_DIM]
        v_o[hh] = v[:, hh * HEAD_DIM:(hh + 1) * HEAD_DIM]


def _gqa_project(x2, mod, gain, p, rope, seg, segt, segk, segtk, tiles_per_sample):
    n, d = x2.shape
    dkv = ATTN_KV_HEADS * HEAD_DIM
    nq, nkv = d // HEAD_DIM, ATTN_KV_HEADS
    heads = lambda h: pl.BlockSpec((h, TM, HEAD_DIM), lambda i: (0, i, 0))
    pos = pl.BlockSpec((TM, LANES), lambda i: (i % tiles_per_sample, 0))
    return pl.pallas_call(
        functools.partial(_gqa_proj_kernel, d=d, dkv=dkv),
        out_shape=(jax.ShapeDtypeStruct((nq, n, HEAD_DIM), BF16), jax.ShapeDtypeStruct((nkv, n, HEAD_DIM), BF16),
                   jax.ShapeDtypeStruct((nkv, n, HEAD_DIM), BF16)),
        grid=(n // TM,),
        in_specs=[
            pl.BlockSpec((TM, d), lambda i: (i, 0)),
            pl.BlockSpec((1, N_MOD, d), lambda i: (_mod_row(i, tiles_per_sample), 0, 0)),
            _const_spec((1, d)),
            _const_spec((d, d + 2 * dkv)),
            _const_spec((1, d)), _const_spec((1, dkv)),
            pos, pos, pos,
            _const_spec((d, LANES)), _const_spec((LANES, d)),
            _const_spec((dkv, LANES)), _const_spec((LANES, dkv)),
        ],
        out_specs=(heads(nq), heads(nkv), heads(nkv)),
        compiler_params=_cparams("arbitrary"),
        name="gqa_project",
    )(x2, mod, gain, p["w_qkv"], p["q_gain"], p["k_gain"], rope[0], rope[1], rope[2], seg, segt, segk, segtk)


def _attn_kernel(q_ref, k_ref, v_ref, *rest, group, kvs):
    o_ref = rest[-1]
    for kv in range(kvs):
        k = k_ref[kv]
        v = v_ref[kv]
        for hh in range(kv * group, (kv + 1) * group):
            s = _dot_nt(q_ref[hh], k)
            mx = jnp.max(s, axis=-1, keepdims=True)
            pr = jnp.exp(s - mx)
            den = jnp.sum(pr, axis=-1, keepdims=True)
            o_ref[hh] = (_dot(pr.astype(BF16), v) / den).astype(BF16)


def _attention(q, k, v, batch, tiles_per_sample, q_tiles, q_tile0, kv_rows, into=None, kvs=2):
    nq, n, _ = q.shape
    nkv = k.shape[0]
    group = nq // nkv
    tt = n // batch
    assert tt % kv_rows == 0
    q_spec = pl.BlockSpec((kvs * group, TM, HEAD_DIM), lambda bb, g, t: (g, bb * tiles_per_sample + q_tile0 + t, 0))
    kv_spec = pl.BlockSpec((kvs, kv_rows, HEAD_DIM), lambda bb, g, t: (g, bb * (tt // kv_rows), 0))
    in_specs = [q_spec, kv_spec, kv_spec]
    args = [q, k, v]
    aliases = {}
    if into is not None:
        in_specs.append(pl.BlockSpec(memory_space=pl.ANY))
        args.append(into)
        aliases = {3: 0}
    return pl.pallas_call(
        functools.partial(_attn_kernel, group=group, kvs=kvs),
        out_shape=jax.ShapeDtypeStruct((nq, n, HEAD_DIM), BF16),
        grid=(batch, nkv // kvs, q_tiles),
        in_specs=in_specs,
        out_specs=q_spec,
        input_output_aliases=aliases,
        compiler_params=_cparams("arbitrary", "arbitrary", "arbitrary"),
        name="gqa_attention",
    )(*args)


def _attn_out_kernel(o_ref, x_ref, m_ref, wo_ref, y_ref):
    o = jnp.concatenate([o_ref[hh] for hh in range(o_ref.shape[0])], axis=1)
    y_ref[...] = x_ref[...] + m_ref[0][2:3] * _dot(o, wo_ref[...])


def _attn_out(o, x2, mod, w_o, tiles_per_sample):
    n, d = x2.shape
    nq = o.shape[0]
    row_spec = pl.BlockSpec((TM, d), lambda i: (i, 0))
    return pl.pallas_call(
        _attn_out_kernel,
        out_shape=jax.ShapeDtypeStruct((n, d), F32),
        grid=(n // TM,),
        in_specs=[
            pl.BlockSpec((nq, TM, HEAD_DIM), lambda i: (0, i, 0)),
            row_spec,
            pl.BlockSpec((1, N_MOD, d), lambda i: (_mod_row(i, tiles_per_sample), 0, 0)),
            _const_spec((d, d)),
        ],
        out_specs=row_spec,
        compiler_params=_cparams("arbitrary"),
        name="gqa_out_proj",
    )(o, x2, mod, w_o)


def _router_kernel(x_ref, m_ref, gain_ref, wr_ref, f_o, afft_o):
    m = m_ref[0]
    f = _norm_mod(x_ref[...], gain_ref[...], m[3:4], m[4:5])
    f_o[...] = f
    logits = _mm(f, wr_ref[...], 3)
    lane = lax.broadcasted_iota(jnp.int32, logits.shape, 1)
    logits = jnp.where(lane < N_EXPERTS, logits, -1e30)
    e = jnp.exp(logits - jnp.max(logits, axis=-1, keepdims=True))
    aff = e / jnp.sum(e, axis=-1, keepdims=True)
    afft_o[0] = aff.T[:N_EXPERTS]


def _moe_router(x2, mod, gain, w_router, batch, tiles_per_sample):
    n, d = x2.shape
    tt = n // batch
    return pl.pallas_call(
        _router_kernel,
        out_shape=(jax.ShapeDtypeStruct((n, d), F32), jax.ShapeDtypeStruct((batch, N_EXPERTS, tt), F32)),
        grid=(n // TM,),
        in_specs=[
            pl.BlockSpec((TM, d), lambda i: (i, 0)),
            pl.BlockSpec((1, N_MOD, d), lambda i: (_mod_row(i, tiles_per_sample), 0, 0)),
            _const_spec((1, d)),
            _const_spec((d, LANES)),
        ],
        out_specs=(pl.BlockSpec((TM, d), lambda i: (i, 0)),
                   pl.BlockSpec((1, N_EXPERTS, TM), lambda i: (i // tiles_per_sample, 0, i % tiles_per_sample))),
        compiler_params=_cparams("arbitrary"),
        name="moe_router",
    )(x2, mod, gain, w_router)


def _prefix_excl(mask, tri_excl):
    xb = mask.astype(BF16)
    carry = jnp.zeros((mask.shape[0], 1), F32)
    outs = []
    for blk in range(mask.shape[1] // LANES):
        piece = xb[:, blk * LANES:(blk + 1) * LANES]
        outs.append(_dot(piece, tri_excl) + carry)
        carry = carry + jnp.sum(piece.astype(F32), axis=1, keepdims=True)
    return jnp.concatenate(outs, axis=1) if len(outs) > 1 else outs[0]


def _top_cap(a, cap, tri_excl):
    ai = lax.bitcast_convert_type(a, jnp.int32)

    def body(i, thr):
        cand = thr | jnp.left_shift(jnp.int32(1), 30 - i)
        cnt = jnp.sum((ai >= cand).astype(F32), axis=1, keepdims=True)
        return jnp.where(cnt >= cap, cand, thr)

    thr = lax.fori_loop(0, 31, body, jnp.zeros((a.shape[0], 1), jnp.int32))
    gt = ai > thr
    eq = ai == thr
    need = cap - jnp.sum(gt.astype(F32), axis=1, keepdims=True)
    sel = jnp.logical_or(gt, jnp.logical_and(eq, _prefix_excl(eq, tri_excl) < need))
    return sel, _prefix_excl(sel, tri_excl)


def _select_kernel(afft_ref, tri_ref, idx_o, gate_o, sel_s, pos_s, *, sets):
    e = pl.program_id(1)

    @pl.when(e == 0)
    def _():
        for off, n, cap, _ in sets:
            sel, pos = _top_cap(afft_ref[0, :, off:off + n], cap, tri_ref[...])
            sel_s[:, off:off + n] = sel.astype(F32)
            pos_s[:, off:off + n] = pos

    for off, n, cap, slot0 in sets:
        a_e = afft_ref[0, pl.ds(e, 1), off:off + n]
        sel_e = sel_s[pl.ds(e, 1), off:off + n]
        pos_e = pos_s[pl.ds(e, 1), off:off + n]
        wl = min(cap, LANES)
        nw = cap // wl
        pos_i = pos_e.astype(jnp.int32)
        lane_of = jnp.bitwise_and(pos_i, wl - 1)
        win_of = jnp.right_shift(pos_i, wl.bit_length() - 1)
        lane_id = lax.broadcasted_iota(jnp.int32, (wl, n), 0)
        onehot = jnp.where(jnp.logical_and(lane_of == lane_id, sel_e > 0), 1.0, 0.0).astype(BF16)
        tok = lax.broadcasted_iota(jnp.int32, (16, n), 1)
        rid = lax.broadcasted_iota(jnp.int32, (16, n), 0)
        a_h, a_m, a_l = _split3(a_e)
        rows = jnp.where(rid == 0, jnp.right_shift(tok, 6).astype(F32),
               jnp.where(rid == 1, jnp.bitwise_and(tok, 63).astype(F32),
               jnp.where(rid == 2, a_h.astype(F32),
               jnp.where(rid == 3, a_m.astype(F32),
               jnp.where(rid == 4, a_l.astype(F32), 0.0)))))
        stacked = jnp.concatenate([jnp.where(win_of == wi, rows, 0.0) for wi in range(nw)], axis=0).astype(BF16)
        res_all = _dot_nt(stacked, onehot)
        for wi in range(nw):
            res = res_all[16 * wi:16 * (wi + 1)]
            lo = slot0 + wi * wl
            idx_o[0, 0, :, lo:lo + wl] = (res[0:1] * 64.0 + res[1:2]).astype(jnp.int32) + off
            gate_o[0, 0, :, lo:lo + wl] = res[2:3] + (res[3:4] + res[4:5])


def _moe_select(afft, sets, n_slots):
    batch, ne, tt = afft.shape
    tri = (jnp.arange(LANES)[:, None] < jnp.arange(LANES)[None, :]).astype(BF16)
    slot_spec = pl.BlockSpec((1, 1, 1, n_slots), lambda bb, e: (bb, e, 0, 0))
    return pl.pallas_call(
        functools.partial(_select_kernel, sets=sets),
        out_shape=(jax.ShapeDtypeStruct((batch, ne, 1, n_slots), jnp.int32),
                   jax.ShapeDtypeStruct((batch, ne, 1, n_slots), F32)),
        grid=(batch, ne),
        in_specs=[pl.BlockSpec((1, ne, tt), lambda bb, e: (bb, 0, 0)), _const_spec((LANES, LANES))],
        out_specs=(slot_spec, slot_spec),
        scratch_shapes=[pltpu.VMEM((ne, tt), F32), pltpu.VMEM((ne, tt), F32)],
        compiler_params=_cparams("arbitrary", "arbitrary"),
        name="moe_select",
    )(afft, tri)


def _gather_kernel(idx_ref, f_ref, xe_o, buf, *, n_slots):
    def body(c, _):
        r = idx_ref[0, 0, 0, c]
        buf[pl.ds(c, 1), :] = f_ref[pl.ds(r, 1), :]
        return 0

    lax.fori_loop(0, n_slots, body, 0, unroll=8)
    xe_o[0, 0] = buf[...].astype(BF16)


def _moe_gather(idx, f, batch):
    n, d = f.shape
    tt = n // batch
    ne, n_slots = idx.shape[1], idx.shape[3]
    return pl.pallas_call(
        functools.partial(_gather_kernel, n_slots=n_slots),
        out_shape=jax.ShapeDtypeStruct((batch, ne, n_slots, d), BF16),
        grid=(batch, ne),
        in_specs=[
            pl.BlockSpec((1, 1, 1, n_slots), lambda bb, e: (bb, e, 0, 0), memory_space=pltpu.SMEM),
            pl.BlockSpec((tt, d), lambda bb, e: (bb, 0)),
        ],
        out_specs=pl.BlockSpec((1, 1, n_slots, d), lambda bb, e: (bb, e, 0, 0)),
        scratch_shapes=[pltpu.VMEM((n_slots, d), F32)],
        compiler_params=_cparams("arbitrary", "arbitrary"),
        name="moe_gather",
    )(idx, f)


def _expert_kernel(xe_ref, wg_ref, wu_ref, wd_ref, ye_o):
    first = pl.program_id(2) == 0
    wg = wg_ref[0, 0].astype(BF16)
    wu = wu_ref[0, 0].astype(BF16)
    wd = wd_ref[0, 0].astype(BF16)
    for i in range(xe_ref.shape[0]):
        x = xe_ref[i, 0]
        h = (_silu(_dot(x, wg)) * _dot(x, wu)).astype(BF16)
        y = _dot(h, wd)
        ye_o[i, 0] = jnp.where(first, y, ye_o[i, 0] + y)


def _moe_experts(xe, w_gate_up, w_down, layer, halves=2, tf=256):
    batch, ne, s, d = xe.shape
    de = w_down.shape[2]
    nf = de // tf
    bh = batch // halves
    tok = pl.BlockSpec((bh, 1, s, d), lambda e, mh, f: (mh, e, 0, 0))
    return pl.pallas_call(
        _expert_kernel,
        out_shape=jax.ShapeDtypeStruct((batch, ne, s, d), F32),
        grid=(ne, halves, nf),
        in_specs=[
            tok,
            pl.BlockSpec((1, 1, d, tf), lambda e, mh, f: (layer, e, 0, f)),
            pl.BlockSpec((1, 1, d, tf), lambda e, mh, f: (layer, e, 0, nf + f)),
            pl.BlockSpec((1, 1, tf, d), lambda e, mh, f: (layer, e, f, 0)),
        ],
        out_specs=tok,
        compiler_params=_cparams("arbitrary", "arbitrary", "arbitrary"),
        name="moe_experts",
    )(xe, w_gate_up, w_gate_up, w_down)


def _combine_kernel(idx_ref, gate_ref, ye_ref, out_o, *, n_slots):
    e = pl.program_id(1)

    @pl.when(e == 0)
    def _():
        out_o[...] = jnp.zeros_like(out_o)

    def body(c, _):
        r = idx_ref[0, 0, 0, c]
        out_o[pl.ds(r, 1), :] += gate_ref[0, 0, 0, c] * ye_ref[0, 0, pl.ds(c, 1), :]
        return 0

    lax.fori_loop(0, n_slots, body, 0, unroll=8)


def _moe_combine(idx, gate, ye, tt):
    batch, ne, n_slots, d = ye.shape
    smem = pl.BlockSpec((1, 1, 1, n_slots), lambda bb, e: (bb, e, 0, 0), memory_space=pltpu.SMEM)
    return pl.pallas_call(
        functools.partial(_combine_kernel, n_slots=n_slots),
        out_shape=jax.ShapeDtypeStruct((batch * tt, d), F32),
        grid=(batch, ne),
        in_specs=[smem, smem, pl.BlockSpec((1, 1, n_slots, d), lambda bb, e: (bb, e, 0, 0))],
        out_specs=pl.BlockSpec((tt, d), lambda bb, e: (bb, 0)),
        compiler_params=_cparams("arbitrary", "arbitrary"),
        name="moe_combine",
    )(idx, gate, ye)


def _residual_kernel(x_ref, y_ref, m_ref, o_ref):
    o_ref[...] = x_ref[...] + m_ref[0][5:6] * y_ref[...]


def _moe_residual(x2, y, mod, tiles_per_sample):
    n, d = x2.shape
    row_spec = pl.BlockSpec((TM, d), lambda i: (i, 0))
    return pl.pallas_call(
        _residual_kernel,
        out_shape=jax.ShapeDtypeStruct((n, d), F32),
        grid=(n // TM,),
        in_specs=[row_spec, row_spec,
                  pl.BlockSpec((1, N_MOD, d), lambda i: (_mod_row(i, tiles_per_sample), 0, 0))],
        out_specs=row_spec,
        compiler_params=_cparams("arbitrary"),
        name="moe_residual",
    )(x2, y, mod)


def _final_kernel(x_ref, g_ref, o_ref):
    x = x_ref[...]
    ms = jnp.mean(x * x, axis=-1, keepdims=True)
    o_ref[0] = (x * lax.rsqrt(ms + NORM_EPS)) * g_ref[...]


def _final_norm(x2, gain, batch, tiles_per_sample):
    n, d = x2.shape
    lat_tiles = tiles_per_sample - 1
    return pl.pallas_call(
        _final_kernel,
        out_shape=jax.ShapeDtypeStruct((batch, lat_tiles * TM, d), F32),
        grid=(batch, lat_tiles),
        in_specs=[pl.BlockSpec((TM, d), lambda bb, t: (bb * tiles_per_sample + 1 + t, 0)), _const_spec((1, d))],
        out_specs=pl.BlockSpec((1, TM, d), lambda bb, t: (bb, t, 0)),
        compiler_params=_cparams("arbitrary", "arbitrary"),
        name="final_norm",
    )(x2, gain)


def _segment_matrices(width):
    heads = jnp.arange(width) // HEAD_DIM
    seg = (heads[:, None] == jnp.arange(LANES)[None, :]).astype(BF16)
    return seg, seg.T


def _rope_tables(seq, ctx_len):
    t = jnp.arange(seq)
    pos = jnp.stack([(t // GRID_W).astype(F32), (t % GRID_W).astype(F32)], axis=-1)
    half = HEAD_DIM // 4
    inv_freq = ROPE_THETA ** (-jnp.arange(0, 2 * half, 2, dtype=F32) / (2 * half))
    ang = pos[:, :, None] * inv_freq
    cos, sin = jnp.cos(ang), jnp.sin(ang)
    zero = jnp.zeros_like(sin)
    per_head = lambda first, second: jnp.concatenate([first, second], axis=-1).reshape(seq, HEAD_DIM)
    cos_h = per_head(cos, cos)
    sin_lo = per_head(-sin, zero)
    sin_hi = per_head(zero, sin)
    pad = lambda tab, fill: jnp.concatenate([jnp.full((ctx_len, HEAD_DIM), fill, F32), tab], axis=0)
    two = lambda tab: jnp.concatenate([tab, tab], axis=1)
    return two(pad(cos_h, 1.0)), two(pad(sin_lo, 0.0)), two(pad(sin_hi, 0.0))


def kernel(x, c, ctx, c_ctx, mod_w, mod_b, norm_mix, norm_ffn, rwkv_mu, rwkv_w_rkv, rwkv_w0, rwkv_w1, rwkv_w2,
           rwkv_a0, rwkv_a1, rwkv_a2, rwkv_g1, rwkv_g2, rwkv_k_k, rwkv_k_a, rwkv_r_k, rwkv_ln_w, rwkv_ln_b,
           rwkv_w_o, attn_w_qkv, attn_q_gain, attn_k_gain, attn_w_o, moe_router, moe_w_gate_up, moe_w_down,
           final_norm):
    batch, seq, d = x.shape
    ctx_len = ctx.shape[1]
    depth = mod_w.shape[0]
    tt = ctx_len + seq
    tps = tt // TM
    assert ctx_len == TM and seq % TM == 0 and CHUNK == HEAD_DIM and batch + 1 <= 16
    nq = d // HEAD_DIM

    x2 = jnp.concatenate([ctx, x], axis=1).reshape(batch * tt, d)
    cc = jnp.zeros((16, d), F32).at[0].set(c_ctx).at[1:batch + 1].set(c)
    mods = _mod_vectors(cc, mod_w, mod_b).reshape(depth, 16, N_MOD, d)
    seg, segt = _segment_matrices(d)
    segk, segtk = _segment_matrices(ATTN_KV_HEADS * HEAD_DIM)
    rope = _rope_tables(seq, ctx_len)
    cap_l = EC_CAPACITY * seq // N_EXPERTS
    cap_c = EC_CAPACITY * ctx_len // N_EXPERTS
    sets = ((ctx_len, seq, cap_l, 0), (0, ctx_len, cap_c, cap_l))
    router_pad = jnp.zeros((depth, d, LANES), F32).at[:, :, :N_EXPERTS].set(moe_router)

    ia = ib = 0
    for i in range(depth):
        mod = mods[i]
        if i % 2 == 0:
            zpad = jnp.zeros((HEAD_DIM, d), F32)
            lora_pad = lambda w: jnp.stack([jnp.concatenate([w[0], zpad], 0), jnp.concatenate([zpad, w[1]], 0)]).astype(BF16)
            p = {
                "mu": rwkv_mu[ia], "w_rkv": rwkv_w_rkv[ia].astype(BF16),
                "w1": jnp.concatenate([rwkv_w1[ia, 0], rwkv_w1[ia, 1]], axis=1).astype(BF16), "w2": lora_pad(rwkv_w2[ia]),
                "w0": rwkv_w0[ia],
                "a1": jnp.concatenate([rwkv_a1[ia, 0], rwkv_a1[ia, 1]], axis=1).astype(BF16), "a2": lora_pad(rwkv_a2[ia]),
                "a0": rwkv_a0[ia],
                "g1": rwkv_g1[ia].astype(BF16), "g2": rwkv_g2[ia].astype(BF16),
                "k_k": rwkv_k_k[ia].reshape(1, d), "k_a": rwkv_k_a[ia].reshape(1, d),
                "r_k": rwkv_r_k[ia].reshape(1, d), "ln_w": rwkv_ln_w[ia].reshape(1, d), "ln_b": rwkv_ln_b[ia].reshape(1, d),
                "w_o": rwkv_w_o[ia].astype(BF16),
            }
            r, v, kk, g, lw, kd, b = _rwkv_inputs(x2, mod, norm_mix[i].reshape(1, d), p, seg, segt, tps)
            yf, yr = _rwkv_scan(r, v, kk, lw, kd, b, batch, ctx_len)
            x2 = _rwkv_readout(yf, yr, r, kd, v, g, x2, mod, p, seg, segt, tps)
            ia += 1
        else:
            p = {
                "w_qkv": attn_w_qkv[ib].astype(BF16),
                "q_gain": jnp.tile(attn_q_gain[ib], nq).reshape(1, d),
                "k_gain": jnp.tile(attn_k_gain[ib], ATTN_KV_HEADS).reshape(1, ATTN_KV_HEADS * HEAD_DIM),
            }
            q, k, v = _gqa_project(x2, mod, norm_mix[i].reshape(1, d), p, rope, seg, segt, segk, segtk, tps)
            o = _attention(q, k, v, batch, tps, tps - 1, 1, tt)
            o = _attention(q, k, v, batch, tps, 1, 0, ctx_len, into=o)
            x2 = _attn_out(o, x2, mod, attn_w_o[ib].astype(BF16), tps)
            ib += 1
        f, afft = _moe_router(x2, mod, norm_ffn[i].reshape(1, d), router_pad[i], batch, tps)
        idx, gate = _moe_select(afft, sets, cap_l + cap_c)
        xe = _moe_gather(idx, f, batch)
        ye = _moe_experts(xe, moe_w_gate_up, moe_w_down, i)
        x2 = _moe_residual(x2, _moe_combine(idx, gate, ye, tt), mod, tps)
    return _final_norm(x2, final_norm.reshape(1, d), batch, tps)
```

```python
import functools

import jax
import jax.numpy as jnp
from jax import lax
from jax.experimental import pallas as pl
from jax.experimental.pallas import tpu as pltpu

F32 = jnp.float32
BF16 = jnp.bfloat16

HEAD_DIM = 64
N_MOD = 6
NORM_EPS = 1e-6
GN_EPS = 64e-5
ROPE_THETA = 10000.0
GRID_W = 64
ATTN_KV_HEADS = 4
N_EXPERTS = 16
EC_CAPACITY = 2
TM = 256
CHUNK = 64
LANES = 128
VMEM_LIMIT = 56 * 1024 * 1024
NEG_EXP_M05 = -0.6065306597126334
LOG2_E = 1.4426950408889634


def _cparams(*sem):
    return pltpu.CompilerParams(dimension_semantics=sem, vmem_limit_bytes=VMEM_LIMIT)


def _split2(x):
    hi = x.astype(BF16)
    lo = (x - hi.astype(F32)).astype(BF16)
    return hi, lo


def _split3(x):
    hi = x.astype(BF16)
    r1 = x - hi.astype(F32)
    mid = r1.astype(BF16)
    lo = (r1 - mid.astype(F32)).astype(BF16)
    return hi, mid, lo


def _dot(a, b):
    return jnp.dot(a, b, preferred_element_type=F32)


def _dot_nt(a, b):
    return lax.dot_general(a, b, (((1,), (1,)), ((), ())), preferred_element_type=F32)


def _dot_tn(a, b):
    return lax.dot_general(a, b, (((0,), (0,)), ((), ())), preferred_element_type=F32)


def _bdot(a, b):
    return _dot(a.astype(BF16), b.astype(BF16))


def _mm(a, b, passes, kind="nn"):
    f = {"nn": _dot, "nt": _dot_nt, "tn": _dot_tn}[kind]
    if passes == 1:
        return f(a.astype(BF16), b.astype(BF16))
    ah, al = _split2(a)
    bh, bl = _split2(b)
    return f(ah, bh) + (f(ah, bl) + f(al, bh))


def _dot_exact_rhs(a, b_bf16):
    h, l = _split2(a)
    return _dot(h, b_bf16) + _dot(l, b_bf16)


def _seg_sum(x, seg, segt):
    s = _dot_exact_rhs(x, seg)
    return _dot_exact_rhs(s, segt)


def _norm_mod(x, gain, shift, scale):
    ms = jnp.mean(x * x, axis=-1, keepdims=True)
    y = x * lax.rsqrt(ms + NORM_EPS)
    return (y * gain) * (1.0 + scale) + shift


def _sigmoid(x):
    return 0.5 * jnp.tanh(0.5 * x) + 0.5


def _silu(x):
    return x * _sigmoid(x)


def _mod_row(i, tiles_per_sample):
    return jnp.where(i % tiles_per_sample == 0, 0, 1 + i // tiles_per_sample)


def _const_spec(shape):
    nd = len(shape)
    return pl.BlockSpec(shape, lambda *_: (0,) * nd)


def _mod_kernel(c_ref, w_ref, b_ref, o_ref):
    s = _silu(c_ref[...])
    o_ref[0] = _mm(s, w_ref[0], 3) + b_ref[0]


def _mod_vectors(cc, mod_w, mod_b):
    depth, d, n = mod_w.shape
    tn = 1536
    return pl.pallas_call(
        _mod_kernel,
        out_shape=jax.ShapeDtypeStruct((depth, 16, n), F32),
        grid=(depth, n // tn),
        in_specs=[
            pl.BlockSpec((16, d), lambda l, j: (0, 0)),
            pl.BlockSpec((1, d, tn), lambda l, j: (l, 0, j)),
            pl.BlockSpec((1, 1, tn), lambda l, j: (l, 0, j)),
        ],
        out_specs=pl.BlockSpec((1, 16, tn), lambda l, j: (l, 0, j)),
        compiler_params=_cparams("arbitrary", "arbitrary"),
        name="mod_vectors",
    )(cc, mod_w, mod_b.reshape(depth, 1, n))


def _rwkv_in_kernel(x_ref, xp_ref, xn_ref, m_ref, gain_ref, mu_ref, wrkv_ref, w1_ref, w2_ref, w0_ref,
                    a1_ref, a2_ref, a0_ref, g1_ref, g2_ref, kk_ref_, ka_ref, seg_ref, segt_ref,
                    r_o, v_o, kk_o, g_o, lw_o, kd_o, b_o, *, tiles_per_sample):
    i = pl.program_id(0)
    j = i % tiles_per_sample
    m = m_ref[0]
    shift, scale = m[0:1], m[1:2]
    gain = gain_ref[...]
    h = _norm_mod(x_ref[...], gain, shift, scale)
    has_prev = (j >= 2).astype(F32)
    has_next = jnp.logical_and(j >= 1, j <= tiles_per_sample - 2).astype(F32)
    hp_row = _norm_mod(xp_ref[7:8, :], gain, shift, scale) * has_prev
    hn_row = _norm_mod(xn_ref[0:1, :], gain, shift, scale) * has_next
    row = lax.broadcasted_iota(jnp.int32, h.shape, 0)
    h_prev = jnp.where(row == 0, hp_row, pltpu.roll(h, 1, 0))
    h_next = jnp.where(row == TM - 1, hn_row, pltpu.roll(h, TM - 1, 0))
    xx = 0.5 * (h_prev + h_next) - h
    mu = mu_ref[...]

    def mix(n):
        return (h + xx * mu[n:n + 1]).astype(BF16)

    r = _dot(mix(0), wrkv_ref[0])
    k = _dot(mix(1), wrkv_ref[1])
    v = _dot(mix(2), wrkv_ref[2])
    tw = jnp.tanh(_dot(mix(3), w1_ref[...])).astype(BF16)
    ua = _dot(mix(4), a1_ref[...]).astype(BF16)
    g = _dot(_sigmoid(_dot(mix(5), g1_ref[...])).astype(BF16), g2_ref[...])
    kk = k * kk_ref_[...]
    n2 = _seg_sum(kk * kk, seg_ref[...], segt_ref[...])
    kk = kk / jnp.maximum(jnp.sqrt(n2), 1e-12)
    r_o[...] = r
    v_o[...] = v
    kk_o[...] = kk
    g_o[...] = g
    ka = ka_ref[...]
    for z in range(2):
        w_pre = w0_ref[z:z + 1, :] + _dot(tw, w2_ref[z])
        lw_o[z] = NEG_EXP_M05 * _sigmoid(w_pre)
        a =_sigmoid(a0_ref[z:z + 1, :] + _dot(ua, a2_ref[z]))
        kd_o[z] = k * (1.0 + (a - 1.0) * ka)
        b_o[z] = kk * a


def _rwkv_inputs(x2, mod, gain, p, seg, segt, tiles_per_sample):
    n, d = x2.shape
    nt = n // TM
    blk8 = TM // 8
    last8 = n // 8 - 1
    row_spec = pl.BlockSpec((TM, d), lambda i: (i, 0))
    dir_spec = pl.BlockSpec((2, TM, d), lambda i: (0, i, 0))
    tok = jax.ShapeDtypeStruct((n, d), F32)
    tok2 = jax.ShapeDtypeStruct((2, n, d), F32)
    return pl.pallas_call(
        functools.partial(_rwkv_in_kernel, tiles_per_sample=tiles_per_sample),
        out_shape=(tok, tok, tok, tok, tok2, tok2, tok2),
        grid=(nt,),
        in_specs=[
            row_spec,
            pl.BlockSpec((8, d), lambda i: (jnp.maximum(i * blk8 - 1, 0), 0)),
            pl.BlockSpec((8, d), lambda i: (jnp.minimum((i + 1) * blk8, last8), 0)),
            pl.BlockSpec((1, N_MOD, d), lambda i: (_mod_row(i, tiles_per_sample), 0, 0)),
            _const_spec((1, d)),
            _const_spec((6, d)),
            _const_spec((3, d, d)),
            _const_spec((d, LANES)),
            _const_spec((2, LANES, d)),
            _const_spec((2, d)),
            _const_spec((d, LANES)),
            _const_spec((2, LANES, d)),
            _const_spec((2, d)),
            _const_spec((d, LANES)),
            _const_spec((LANES, d)),
            _const_spec((1, d)),
            _const_spec((1, d)),
            _const_spec((d, LANES)),
            _const_spec((LANES, d)),
        ],
        out_specs=(row_spec, row_spec, row_spec, row_spec, dir_spec, dir_spec, dir_spec),
        compiler_params=_cparams("arbitrary"),
        name="rwkv_inputs",
    )(x2, x2, x2, mod, gain, p["mu"], p["w_rkv"], p["w1"], p["w2"], p["w0"], p["a1"], p["a2"], p["a0"],
      p["g1"], p["g2"], p["k_k"], p["k_a"], seg, segt)


def _scan_consts(reverse):
    sgn = -1 if reverse else 1
    n2 = 2 * CHUNK
    row = lax.broadcasted_iota(jnp.int32, (n2, n2), 0)
    col = lax.broadcasted_iota(jnp.int32, (n2, n2), 1)
    same = (row // CHUNK) == (col // CHUNK)
    dt = (row % CHUNK - col % CHUNK) * sgn
    rc = lax.broadcasted_iota(jnp.int32, (CHUNK, CHUNK), 0)
    cc = lax.broadcasted_iota(jnp.int32, (CHUNK, CHUNK), 1)
    return {
        "same": same,
        "strict": jnp.logical_and(same, dt > 0),
        "incl": jnp.logical_and(same, dt >= 0),
        "eye": (row == col).astype(F32),
        "tri": ((rc - cc) * sgn >= 0).astype(BF16),
        "head0": lax.broadcasted_iota(jnp.int32, (CHUNK, LANES), 1) < HEAD_DIM,
    }


def _scan_units(units, passes):
    n2 = 2 * CHUNK
    every = lambda f: [f(u) for u in units]

    def stack(u, x):
        return jnp.concatenate([jnp.where(u["c"]["head0"], x, 0.0), jnp.where(u["c"]["head0"], 0.0, x)], axis=0)

    def prep(u):
        cl = _dot_exact_rhs_left(u["c"]["tri"], u["lw"])
        tot = jnp.sum(u["lw"], axis=0, keepdims=True)
        e_ncl = jnp.exp(-cl)
        e_end = jnp.exp(tot - cl)
        u["q2"] = jnp.concatenate([stack(u, u["kk"] * jnp.exp(cl - u["lw"])), stack(u, u["r"] * jnp.exp(cl))], axis=0)
        u["k2"] = jnp.concatenate([stack(u, u["kd"] * e_ncl), stack(u, u["b"] * e_ncl)], axis=0)
        u["ket"] = jnp.concatenate([u["kd"] * e_end, -(u["b"] * e_end)], axis=0).T
        u["g_col"] = jnp.sum(jnp.where(u["c"]["eye"] > 0, jnp.exp(tot), 0.0), axis=1, keepdims=True)
        u["vs"] = stack(u, u["v"])

    every(prep)
    a_all = every(lambda u: _mm(u["q2"], u["k2"], passes["a"], "nt"))
    qm = every(lambda u: _mm(u["q2"], u["m0"], passes["qm"]))
    for u, a in zip(units, a_all):
        c = u["c"]
        u["l_kk"] = jnp.where(c["strict"], a[:n2, :n2], 0.0)
        u["l_rk"] = jnp.where(c["incl"], a[n2:, :n2], 0.0)
        u["l_rb"] = jnp.where(c["incl"], a[n2:, n2:], 0.0)
        u["pw"] = -jnp.where(c["strict"], a[:n2, n2:], 0.0)
        u["inv"] = c["eye"] + u["pw"]
    lv = every(lambda u: _mm(u["l_kk"], u["vs"], passes["lv"]))
    for _ in range(CHUNK.bit_length() - 2):
        pw = every(lambda u: _mm(u["pw"], u["pw"], passes["inv"]))
        for u, x in zip(units, pw):
            u["pw"] = x
        upd = every(lambda u: _mm(u["inv"], u["pw"], passes["inv"]))
        for u, x in zip(units, upd):
            u["inv"] = u["inv"] + x
    us = [_mm(u["inv"], q[:n2] + t, passes["us"]) for u, q, t in zip(units, qm, lv)]
    ys = [q[n2:] + _mm(jnp.concatenate([u["l_rk"], -u["l_rb"]], axis=1), jnp.concatenate([u["vs"], s], axis=0), passes["ys"])
          for u, q, s in zip(units, qm, us)]
    new = [_mm(u["ket"], jnp.concatenate([u["v"], s[:CHUNK] + s[CHUNK:]], axis=0), passes["upd"]) for u, s in zip(units, us)]
    ms = [u["m0"] * u["g_col"] + jnp.where(u["c"]["same"], x, 0.0) for u, x in zip(units, new)]
    return [y[:CHUNK] + y[CHUNK:] for y in ys], ms


def _dot_exact_rhs_left(a_bf16, b):
    h, m, l = _split3(b)
    return _dot(a_bf16, h) + (_dot(a_bf16, m) + _dot(a_bf16, l))


SCAN_PASSES = {"a": 1, "qm": 1, "lv": 1, "inv": 1, "us": 1, "ys": 1, "upd": 1}


def _scan_kernel(rf_ref, vf_ref, kkf_ref, rr_ref, vr_ref, kkr_ref, lwf_ref, kdf_ref, bf_ref, lwr_ref, kdr_ref,
                 br_ref, yf_ref, yr_ref, m_ref, *, pairs):
    @pl.when(pl.program_id(2) == 0)
    def _():
        m_ref[...] = jnp.zeros_like(m_ref)

    dirs = (
        (_scan_consts(False), rf_ref, vf_ref, kkf_ref, lwf_ref, kdf_ref, bf_ref, yf_ref),
        (_scan_consts(True), rr_ref, vr_ref, kkr_ref, lwr_ref, kdr_ref, br_ref, yr_ref),
    )
    units = []
    for z, (consts, r_ref, v_ref, kk_ref, lw_ref, kd_ref, b_ref, _) in enumerate(dirs):
        for p in range(pairs):
            sl = slice(p * LANES, (p + 1) * LANES)
            units.append({"c": consts, "r": r_ref[:, sl], "v": v_ref[:, sl], "kk": kk_ref[:, sl], "lw": lw_ref[0, :, sl],
                          "kd": kd_ref[0, :, sl], "b": b_ref[0, :, sl], "m0": m_ref[z * pairs + p]})
    ys, ms = _scan_units(units, SCAN_PASSES)
    for i, (y, m1) in enumerate(zip(ys, ms)):
        z, p = divmod(i, pairs)
        dirs[z][-1][0, :, p * LANES:(p + 1) * LANES] = y
        m_ref[i] = m1


def _rwkv_scan(r, v, kk, lw, kd, b, batch, ctx_len, pairs=8):
    n, d = r.shape
    tt = n // batch
    nch = tt // CHUNK
    nch_ctx = ctx_len // CHUNK
    width = pairs * LANES
    groups = d // width

    fwd_row = lambda bb, c: bb * nch + c
    rev_row = lambda bb, c: bb * nch + jnp.where(c < nch_ctx, nch_ctx - 1 - c, nch + nch_ctx - 1 - c)
    shared = lambda row: pl.BlockSpec((CHUNK, width), lambda bb, g, c: (row(bb, c), g))
    per_dir = lambda z, row: pl.BlockSpec((1, CHUNK, width), lambda bb, g, c: (z, row(bb, c), g))
    return pl.pallas_call(
        functools.partial(_scan_kernel, pairs=pairs),
        out_shape=(jax.ShapeDtypeStruct((1, n, d), F32), jax.ShapeDtypeStruct((1, n, d), F32)),
        grid=(batch, groups, nch),
        in_specs=[shared(fwd_row)] * 3 + [shared(rev_row)] * 3 + [per_dir(0, fwd_row)] * 3 + [per_dir(1, rev_row)] * 3,
        out_specs=(per_dir(0, fwd_row), per_dir(0, rev_row)),
        scratch_shapes=[pltpu.VMEM((2 * pairs, LANES, LANES), F32)],
        compiler_params=_cparams("arbitrary", "arbitrary", "arbitrary"),
        name="rwkv_scan",
    )(r, v, kk, r, v, kk, lw, kd, b, lw, kd, b)


def _rwkv_out_kernel(y0_ref, y1_ref, r_ref, kd0_ref, kd1_ref, v_ref, g_ref, x_ref, m_ref, rk_ref, lnw_ref,
                     lnb_ref, wo_ref, seg_ref, segt_ref, o_ref):
    seg, segt = seg_ref[...], segt_ref[...]
    y = y0_ref[0] + y1_ref[0]
    mean = _seg_sum(y, seg, segt) * (1.0 / HEAD_DIM)
    dy = y - mean
    var = _seg_sum(dy * dy, seg, segt) * (1.0 / HEAD_DIM)
    yn = (dy * lax.rsqrt(var + GN_EPS)) * lnw_ref[...] + lnb_ref[...]
    bonus = _seg_sum(r_ref[...] * (kd0_ref[0] + kd1_ref[0]) * rk_ref[...], seg, segt)
    out = (yn + bonus * v_ref[...]) * g_ref[...]
    o = _dot(out.astype(BF16), wo_ref[...])
    o_ref[...] = x_ref[...] + m_ref[0][2:3] * o


def _rwkv_readout(yf, yr, r, kd, v, g, x2, mod, p, seg, segt, tiles_per_sample):
    n, d = x2.shape
    row_spec = pl.BlockSpec((TM, d), lambda i: (i, 0))
    return pl.pallas_call(
        _rwkv_out_kernel,
        out_shape=jax.ShapeDtypeStruct((n, d), F32),
        grid=(n // TM,),
        in_specs=[
            pl.BlockSpec((1, TM, d), lambda i: (0, i, 0)),
            pl.BlockSpec((1, TM, d), lambda i: (0, i, 0)),
            row_spec,
            pl.BlockSpec((1, TM, d), lambda i: (0, i, 0)),
            pl.BlockSpec((1, TM, d), lambda i: (1, i, 0)),
            row_spec, row_spec, row_spec,
            pl.BlockSpec((1, N_MOD, d), lambda i: (_mod_row(i, tiles_per_sample), 0, 0)),
            _const_spec((1, d)), _const_spec((1, d)), _const_spec((1, d)),
            _const_spec((d, d)),
            _const_spec((d, LANES)), _const_spec((LANES, d)),
        ],
        out_specs=row_spec,
        compiler_params=_cparams("arbitrary"),
        name="rwkv_readout",
    )(yf, yr, r, kd, kd, v, g, x2, mod, p["r_k"], p["ln_w"], p["ln_b"], p["w_o"], seg, segt)


def _rope(x, cos, sin_lo, sin_hi):
    w = x.shape[1]
    reps = w // LANES
    tile = lambda t: jnp.concatenate([t] * reps, axis=1) if reps > 1 else t
    half = HEAD_DIM // 4
    return x * tile(cos) + pltpu.roll(x, w - half, 1) * tile(sin_lo) + pltpu.roll(x, half, 1) * tile(sin_hi)


def _gqa_proj_kernel(x_ref, m_ref, gain_ref, w_ref, qg_ref, kg_ref, cos_ref, slo_ref, shi_ref, seg_ref, segt_ref,
                     segk_ref, segtk_ref, q_o, k_o, v_o, *, d, dkv):
    m = m_ref[0]
    h = _norm_mod(x_ref[...], gain_ref[...], m[0:1], m[1:2]).astype(BF16)
    qkv = _dot(h, w_ref[...])
    q, k, v = qkv[:, :d], qkv[:, d:d + dkv], qkv[:, d + dkv:]
    cos, slo, shi = cos_ref[...], slo_ref[...], shi_ref[...]
    qms = _seg_sum(q * q, seg_ref[...], segt_ref[...]) * (1.0 / HEAD_DIM)
    q = (q * lax.rsqrt(qms + NORM_EPS)) * qg_ref[...]
    kms = _seg_sum(k * k, segk_ref[...], segtk_ref[...]) * (1.0 / HEAD_DIM)
    k = (k * lax.rsqrt(kms + NORM_EPS)) * kg_ref[...]
    q = (_rope(q, cos, slo, shi) * (HEAD_DIM ** -0.5 * LOG2_E)).astype(BF16)
    k = _rope(k, cos, slo, shi).astype(BF16)
    v = v.astype(BF16)
    lane = lax.broadcasted_iota(jnp.int32, (v.shape[0], LANES - HEAD_DIM), 1)
    ones_pad = jnp.where(lane == 0, 1.0, 0.0).astype(BF16)
    for hh in range(d // HEAD_DIM):
        q_o[hh] = q[:, hh * HEAD_DIM:(hh + 1) * HEAD_DIM]
    for hh in range(dkv // HEAD_DIM):
        k_o[hh] = k[:, hh * HEAD_DIM:(hh + 1) * HEAD_DIM]
        v_o[hh] = jnp.concatenate([v[:, hh * HEAD_DIM:(hh + 1) * HEAD_DIM], ones_pad], axis=1)


def _gqa_project(x2, mod, gain, p, rope, seg, segt, segk, segtk, tiles_per_sample):
    n, d = x2.shape
    dkv = ATTN_KV_HEADS * HEAD_DIM
    nq, nkv = d // HEAD_DIM, ATTN_KV_HEADS
    heads = lambda h: pl.BlockSpec((h, TM, HEAD_DIM), lambda i: (0, i, 0))
    pos = pl.BlockSpec((TM, LANES), lambda i: (i % tiles_per_sample, 0))
    return pl.pallas_call(
        functools.partial(_gqa_proj_kernel, d=d, dkv=dkv),
        out_shape=(jax.ShapeDtypeStruct((nq, n, HEAD_DIM), BF16), jax.ShapeDtypeStruct((nkv, n, HEAD_DIM), BF16),
                   jax.ShapeDtypeStruct((nkv, n, LANES), BF16)),
        grid=(n // TM,),
        in_specs=[
            pl.BlockSpec((TM, d), lambda i: (i, 0)),
            pl.BlockSpec((1, N_MOD, d), lambda i: (_mod_row(i, tiles_per_sample), 0, 0)),
            _const_spec((1, d)),
            _const_spec((d, d + 2 * dkv)),
            _const_spec((1, d)), _const_spec((1, dkv)),
            pos, pos, pos,
            _const_spec((d, LANES)), _const_spec((LANES, d)),
            _const_spec((dkv, LANES)), _const_spec((LANES, dkv)),
        ],
        out_specs=(heads(nq), heads(nkv), pl.BlockSpec((nkv, TM, LANES), lambda i: (0, i, 0))),
        compiler_params=_cparams("arbitrary"),
        name="gqa_project",
    )(x2, mod, gain, p["w_qkv"], p["q_gain"], p["k_gain"], rope[0], rope[1], rope[2], seg, segt, segk, segtk)


def _attn_kernel(q_ref, k_ref, v_ref, *rest, group, kvs):
    o_ref = rest[-1]
    for kv in range(kvs):
        k = k_ref[kv]
        v = v_ref[kv]
        for hh in range(kv * group, (kv + 1) * group):
            s = _dot_nt(q_ref[hh], k)
            pr = jnp.exp2(s - jnp.max(s, axis=-1, keepdims=True))
            oe = _dot(pr.astype(BF16), v)
            o_ref[hh] = (oe[:, :HEAD_DIM] / oe[:, HEAD_DIM:HEAD_DIM + 1]).astype(BF16)


def _attention(q, k, v, batch, tiles_per_sample, q_tiles, q_tile0, kv_rows, into=None, kvs=2):
    nq, n, _ = q.shape
    nkv = k.shape[0]
    group = nq // nkv
    tt = n // batch
    assert tt % kv_rows == 0
    q_spec = pl.BlockSpec((kvs * group, TM, HEAD_DIM), lambda bb, g, t: (g, bb * tiles_per_sample + q_tile0 + t, 0))
    kv_spec = pl.BlockSpec((kvs, kv_rows, HEAD_DIM), lambda bb, g, t: (g, bb * (tt // kv_rows), 0))
    v_spec = pl.BlockSpec((kvs, kv_rows, LANES), lambda bb, g, t: (g, bb * (tt // kv_rows), 0))
    in_specs = [q_spec, kv_spec, v_spec]
    args = [q, k, v]
    aliases = {}
    if into is not None:
        in_specs.append(pl.BlockSpec(memory_space=pl.ANY))
        args.append(into)
        aliases = {3: 0}
    return pl.pallas_call(
        functools.partial(_attn_kernel, group=group, kvs=kvs),
        out_shape=jax.ShapeDtypeStruct((nq, n, HEAD_DIM), BF16),
        grid=(batch, nkv // kvs, q_tiles),
        in_specs=in_specs,
        out_specs=q_spec,
        input_output_aliases=aliases,
        compiler_params=_cparams("arbitrary", "arbitrary", "arbitrary"),
        name="gqa_attention",
    )(*args)


def _attn_out_kernel(o_ref, x_ref, m_ref, wo_ref, y_ref):
    o = jnp.concatenate([o_ref[hh] for hh in range(o_ref.shape[0])], axis=1)
    y_ref[...] = x_ref[...] + m_ref[0][2:3] * _dot(o, wo_ref[...])


def _attn_out(o, x2, mod, w_o, tiles_per_sample):
    n, d = x2.shape
    nq = o.shape[0]
    row_spec = pl.BlockSpec((TM, d), lambda i: (i, 0))
    return pl.pallas_call(
        _attn_out_kernel,
        out_shape=jax.ShapeDtypeStruct((n, d), F32),
        grid=(n // TM,),
        in_specs=[
            pl.BlockSpec((nq, TM, HEAD_DIM), lambda i: (0, i, 0)),
            row_spec,
            pl.BlockSpec((1, N_MOD, d), lambda i: (_mod_row(i, tiles_per_sample), 0, 0)),
            _const_spec((d, d)),
        ],
        out_specs=row_spec,
        compiler_params=_cparams("arbitrary"),
        name="gqa_out_proj",
    )(o, x2, mod, w_o)


def _router_kernel(x_ref, m_ref, gain_ref, wr_ref, f_o, afft_o):
    m = m_ref[0]
    f = _norm_mod(x_ref[...], gain_ref[...], m[3:4], m[4:5])
    f_o[...] = f
    logits = _mm(f, wr_ref[...], 3)
    lane = lax.broadcasted_iota(jnp.int32, logits.shape, 1)
    logits = jnp.where(lane < N_EXPERTS, logits, -1e30)
    e = jnp.exp(logits - jnp.max(logits, axis=-1, keepdims=True))
    aff = e / jnp.sum(e, axis=-1, keepdims=True)
    afft_o[0] = aff.T[:N_EXPERTS]


def _moe_router(x2, mod, gain, w_router, batch, tiles_per_sample):
    n, d = x2.shape
    tt = n // batch
    return pl.pallas_call(
        _router_kernel,
        out_shape=(jax.ShapeDtypeStruct((n, d), F32), jax.ShapeDtypeStruct((batch, N_EXPERTS, tt), F32)),
        grid=(n // TM,),
        in_specs=[
            pl.BlockSpec((TM, d), lambda i: (i, 0)),
            pl.BlockSpec((1, N_MOD, d), lambda i: (_mod_row(i, tiles_per_sample), 0, 0)),
            _const_spec((1, d)),
            _const_spec((d, LANES)),
        ],
        out_specs=(pl.BlockSpec((TM, d), lambda i: (i, 0)),
                   pl.BlockSpec((1, N_EXPERTS, TM), lambda i: (i // tiles_per_sample, 0, i % tiles_per_sample))),
        compiler_params=_cparams("arbitrary"),
        name="moe_router",
    )(x2, mod, gain, w_router)


def _prefix_excl(mask, tri_excl):
    xb = mask.astype(BF16)
    carry = jnp.zeros((mask.shape[0], 1), F32)
    outs = []
    for blk in range(mask.shape[1] // LANES):
        piece = xb[:, blk * LANES:(blk + 1) * LANES]
        outs.append(_dot(piece, tri_excl) + carry)
        carry = carry + jnp.sum(piece.astype(F32), axis=1, keepdims=True)
    return jnp.concatenate(outs, axis=1) if len(outs) > 1 else outs[0]


def _top_cap(a, cap, tri_excl):
    ai = lax.bitcast_convert_type(a, jnp.int32)

    def body(i, thr):
        cand = thr | jnp.left_shift(jnp.int32(1), 30 - i)
        cnt = jnp.sum((ai >= cand).astype(F32), axis=1, keepdims=True)
        return jnp.where(cnt >= cap, cand, thr)

    thr = lax.fori_loop(0, 31, body, jnp.zeros((a.shape[0], 1), jnp.int32))
    gt = ai > thr
    eq = ai == thr
    need = cap - jnp.sum(gt.astype(F32), axis=1, keepdims=True)
    sel = jnp.logical_or(gt, jnp.logical_and(eq, _prefix_excl(eq, tri_excl) < need))
    return sel, _prefix_excl(sel, tri_excl)


def _select_kernel(afft_ref, tri_ref, idx_o, gate_o, sel_s, pos_s, *, sets):
    e = pl.program_id(1)

    @pl.when(e == 0)
    def _():
        for off, n, cap, _ in sets:
            sel, pos = _top_cap(afft_ref[0, :, off:off + n], cap, tri_ref[...])
            sel_s[:, off:off + n] = sel.astype(F32)
            pos_s[:, off:off + n] = pos

    for off, n, cap, slot0 in sets:
        a_e = afft_ref[0, pl.ds(e, 1), off:off + n]
        sel_e = sel_s[pl.ds(e, 1), off:off + n]
        pos_e = pos_s[pl.ds(e, 1), off:off + n]
        wl = min(cap, LANES)
        nw = cap // wl
        pos_i = pos_e.astype(jnp.int32)
        lane_of = jnp.bitwise_and(pos_i, wl - 1)
        win_of = jnp.right_shift(pos_i, wl.bit_length() - 1)
        lane_id = lax.broadcasted_iota(jnp.int32, (wl, n), 0)
        onehot = jnp.where(jnp.logical_and(lane_of == lane_id, sel_e > 0), 1.0, 0.0).astype(BF16)
        tok = lax.broadcasted_iota(jnp.int32, (16, n), 1)
        rid = lax.broadcasted_iota(jnp.int32, (16, n), 0)
        a_h, a_m, a_l = _split3(a_e)
        rows = jnp.where(rid == 0, jnp.right_shift(tok, 6).astype(F32),
               jnp.where(rid == 1, jnp.bitwise_and(tok, 63).astype(F32),
               jnp.where(rid == 2, a_h.astype(F32),
               jnp.where(rid == 3, a_m.astype(F32),
               jnp.where(rid == 4, a_l.astype(F32), 0.0)))))
        stacked = jnp.concatenate([jnp.where(win_of == wi, rows, 0.0) for wi in range(nw)], axis=0).astype(BF16)
        res_all = _dot_nt(stacked, onehot)
        for wi in range(nw):
            res = res_all[16 * wi:16 * (wi + 1)]
            lo = slot0 + wi * wl
            idx_o[0, 0, :, lo:lo + wl] = (res[0:1] * 64.0 + res[1:2]).astype(jnp.int32) + off
            gate_o[0, 0, :, lo:lo + wl] = res[2:3] + (res[3:4] + res[4:5])


def _moe_select(afft, sets, n_slots):
    batch, ne, tt = afft.shape
    tri = (jnp.arange(LANES)[:, None] < jnp.arange(LANES)[None, :]).astype(BF16)
    slot_spec = pl.BlockSpec((1, 1, 1, n_slots), lambda bb, e: (bb, e, 0, 0))
    return pl.pallas_call(
        functools.partial(_select_kernel, sets=sets),
        out_shape=(jax.ShapeDtypeStruct((batch, ne, 1, n_slots), jnp.int32),
                   jax.ShapeDtypeStruct((batch, ne, 1, n_slots), F32)),
        grid=(batch, ne),
        in_specs=[pl.BlockSpec((1, ne, tt), lambda bb, e: (bb, 0, 0)), _const_spec((LANES, LANES))],
        out_specs=(slot_spec, slot_spec),
        scratch_shapes=[pltpu.VMEM((ne, tt), F32), pltpu.VMEM((ne, tt), F32)],
        compiler_params=_cparams("arbitrary", "arbitrary"),
        name="moe_select",
    )(afft, tri)


def _gather_kernel(idx_ref, f_ref, xe_o, buf, *, n_slots):
    def body(c, _):
        r = idx_ref[0, 0, 0, c]
        buf[pl.ds(c, 1), :] = f_ref[pl.ds(r, 1), :]
        return 0

    lax.fori_loop(0, n_slots, body, 0, unroll=8)
    xe_o[0, 0] = buf[...].astype(BF16)


def _moe_gather(idx, f, batch):
    n, d = f.shape
    tt = n // batch
    ne, n_slots = idx.shape[1], idx.shape[3]
    return pl.pallas_call(
        functools.partial(_gather_kernel, n_slots=n_slots),
        out_shape=jax.ShapeDtypeStruct((batch, ne, n_slots, d), BF16),
        grid=(batch, ne),
        in_specs=[
            pl.BlockSpec((1, 1, 1, n_slots), lambda bb, e: (bb, e, 0, 0), memory_space=pltpu.SMEM),
            pl.BlockSpec((tt, d), lambda bb, e: (bb, 0)),
        ],
        out_specs=pl.BlockSpec((1, 1, n_slots, d), lambda bb, e: (bb, e, 0, 0)),
        scratch_shapes=[pltpu.VMEM((n_slots, d), F32)],
        compiler_params=_cparams("arbitrary", "arbitrary"),
        name="moe_gather",
    )(idx, f)


def _expert_kernel(xe_ref, wg_ref, wu_ref, wd_ref, ye_o):
    first = pl.program_id(2) == 0
    wg = wg_ref[0, 0].astype(BF16)
    wu = wu_ref[0, 0].astype(BF16)
    wd = wd_ref[0, 0].astype(BF16)
    for i in range(xe_ref.shape[0]):
        x = xe_ref[i, 0]
        h = (_silu(_dot(x, wg)) * _dot(x, wu)).astype(BF16)
        y = _dot(h, wd)
        ye_o[i, 0] = jnp.where(first, y, ye_o[i, 0] + y)


def _moe_experts(xe, w_gate_up, w_down, layer, halves=2, tf=256):
    batch, ne, s, d = xe.shape
    de = w_down.shape[2]
    nf = de // tf
    bh = batch // halves
    tok = pl.BlockSpec((bh, 1, s, d), lambda e, mh, f: (mh, e, 0, 0))
    return pl.pallas_call(
        _expert_kernel,
        out_shape=jax.ShapeDtypeStruct((batch, ne, s, d), F32),
        grid=(ne, halves, nf),
        in_specs=[
            tok,
            pl.BlockSpec((1, 1, d, tf), lambda e, mh, f: (layer, e, 0, f)),
            pl.BlockSpec((1, 1, d, tf), lambda e, mh, f: (layer, e, 0, nf + f)),
            pl.BlockSpec((1, 1, tf, d), lambda e, mh, f: (layer, e, f, 0)),
        ],
        out_specs=tok,
        compiler_params=_cparams("arbitrary", "arbitrary", "arbitrary"),
        name="moe_experts",
    )(xe, w_gate_up, w_gate_up, w_down)


def _combine_kernel(idx_ref, gate_ref, ye_ref, out_o, *, n_slots):
    e = pl.program_id(1)

    @pl.when(e == 0)
    def _():
        out_o[...] = jnp.zeros_like(out_o)

    def body(c, _):
        r = idx_ref[0, 0, 0, c]
        out_o[pl.ds(r, 1), :] += gate_ref[0, 0, 0, c] * ye_ref[0, 0, pl.ds(c, 1), :]
        return 0

    lax.fori_loop(0, n_slots, body, 0, unroll=8)


def _moe_combine(idx, gate, ye, tt):
    batch, ne, n_slots, d = ye.shape
    smem = pl.BlockSpec((1, 1, 1, n_slots), lambda bb, e: (bb, e, 0, 0), memory_space=pltpu.SMEM)
    return pl.pallas_call(
        functools.partial(_combine_kernel, n_slots=n_slots),
        out_shape=jax.ShapeDtypeStruct((batch * tt, d), F32),
        grid=(batch, ne),
        in_specs=[smem, smem, pl.BlockSpec((1, 1, n_slots, d), lambda bb, e: (bb, e, 0, 0))],
        out_specs=pl.BlockSpec((tt, d), lambda bb, e: (bb, 0)),
        compiler_params=_cparams("arbitrary", "arbitrary"),
        name="moe_combine",
    )(idx, gate, ye)


def _residual_kernel(x_ref, y_ref, m_ref, o_ref):
    o_ref[...] = x_ref[...] + m_ref[0][5:6] * y_ref[...]


def _moe_residual(x2, y, mod, tiles_per_sample):
    n, d = x2.shape
    row_spec = pl.BlockSpec((TM, d), lambda i: (i, 0))
    return pl.pallas_call(
        _residual_kernel,
        out_shape=jax.ShapeDtypeStruct((n, d), F32),
        grid=(n // TM,),
        in_specs=[row_spec, row_spec,
                  pl.BlockSpec((1, N_MOD, d), lambda i: (_mod_row(i, tiles_per_sample), 0, 0))],
        out_specs=row_spec,
        compiler_params=_cparams("arbitrary"),
        name="moe_residual",
    )(x2, y, mod)


def _final_kernel(x_ref, g_ref, o_ref):
    x = x_ref[...]
    ms = jnp.mean(x * x, axis=-1, keepdims=True)
    o_ref[0] = (x * lax.rsqrt(ms + NORM_EPS)) * g_ref[...]


def _final_norm(x2, gain, batch, tiles_per_sample):
    n, d = x2.shape
    lat_tiles = tiles_per_sample - 1
    return pl.pallas_call(
        _final_kernel,
        out_shape=jax.ShapeDtypeStruct((batch, lat_tiles * TM, d), F32),
        grid=(batch, lat_tiles),
        in_specs=[pl.BlockSpec((TM, d), lambda bb, t: (bb * tiles_per_sample + 1 + t, 0)), _const_spec((1, d))],
        out_specs=pl.BlockSpec((1, TM, d), lambda bb, t: (bb, t, 0)),
        compiler_params=_cparams("arbitrary", "arbitrary"),
        name="final_norm",
    )(x2, gain)


def _segment_matrices(width):
    heads = jnp.arange(width) // HEAD_DIM
    seg = (heads[:, None] == jnp.arange(LANES)[None, :]).astype(BF16)
    return seg, seg.T


def _rope_tables(seq, ctx_len):
    t = jnp.arange(seq)
    pos = jnp.stack([(t // GRID_W).astype(F32), (t % GRID_W).astype(F32)], axis=-1)
    half = HEAD_DIM // 4
    inv_freq = ROPE_THETA ** (-jnp.arange(0, 2 * half, 2, dtype=F32) / (2 * half))
    ang = pos[:, :, None] * inv_freq
    cos, sin = jnp.cos(ang), jnp.sin(ang)
    zero = jnp.zeros_like(sin)
    per_head = lambda first, second: jnp.concatenate([first, second], axis=-1).reshape(seq, HEAD_DIM)
    cos_h = per_head(cos, cos)
    sin_lo = per_head(-sin, zero)
    sin_hi = per_head(zero, sin)
    pad = lambda tab, fill: jnp.concatenate([jnp.full((ctx_len, HEAD_DIM), fill, F32), tab], axis=0)
    two = lambda tab: jnp.concatenate([tab, tab], axis=1)
    return two(pad(cos_h, 1.0)), two(pad(sin_lo, 0.0)), two(pad(sin_hi, 0.0))


def kernel(x, c, ctx, c_ctx, mod_w, mod_b, norm_mix, norm_ffn, rwkv_mu, rwkv_w_rkv, rwkv_w0, rwkv_w1, rwkv_w2,
           rwkv_a0, rwkv_a1, rwkv_a2, rwkv_g1, rwkv_g2, rwkv_k_k, rwkv_k_a, rwkv_r_k, rwkv_ln_w, rwkv_ln_b,
           rwkv_w_o, attn_w_qkv, attn_q_gain, attn_k_gain, attn_w_o, moe_router, moe_w_gate_up, moe_w_down,
           final_norm):
    batch, seq, d = x.shape
    ctx_len = ctx.shape[1]
    depth = mod_w.shape[0]
    tt = ctx_len + seq
    tps = tt // TM
    assert ctx_len == TM and seq % TM == 0 and CHUNK == HEAD_DIM and batch + 1 <= 16
    nq = d // HEAD_DIM

    x2 = jnp.concatenate([ctx, x], axis=1).reshape(batch * tt, d)
    cc = jnp.zeros((16, d), F32).at[0].set(c_ctx).at[1:batch + 1].set(c)
    mods = _mod_vectors(cc, mod_w, mod_b).reshape(depth, 16, N_MOD, d)
    seg, segt = _segment_matrices(d)
    segk, segtk = _segment_matrices(ATTN_KV_HEADS * HEAD_DIM)
    rope = _rope_tables(seq, ctx_len)
    cap_l = EC_CAPACITY * seq // N_EXPERTS
    cap_c = EC_CAPACITY * ctx_len // N_EXPERTS
    sets = ((ctx_len, seq, cap_l, 0), (0, ctx_len, cap_c, cap_l))
    router_pad = jnp.zeros((depth, d, LANES), F32).at[:, :, :N_EXPERTS].set(moe_router)

    ia = ib = 0
    for i in range(depth):
        mod = mods[i]
        if i % 2 == 0:
            zpad = jnp.zeros((HEAD_DIM, d), F32)
            lora_pad = lambda w: jnp.stack([jnp.concatenate([w[0], zpad], 0), jnp.concatenate([zpad, w[1]], 0)]).astype(BF16)
            p = {
                "mu": rwkv_mu[ia], "w_rkv": rwkv_w_rkv[ia].astype(BF16),
                "w1": jnp.concatenate([rwkv_w1[ia, 0], rwkv_w1[ia, 1]], axis=1).astype(BF16), "w2": lora_pad(rwkv_w2[ia]),
                "w0": rwkv_w0[ia],
                "a1": jnp.concatenate([rwkv_a1[ia, 0], rwkv_a1[ia, 1]], axis=1).astype(BF16), "a2": lora_pad(rwkv_a2[ia]),
                "a0": rwkv_a0[ia],
                "g1": rwkv_g1[ia].astype(BF16), "g2": rwkv_g2[ia].astype(BF16),
                "k_k": rwkv_k_k[ia].reshape(1, d), "k_a": rwkv_k_a[ia].reshape(1, d),
                "r_k": rwkv_r_k[ia].reshape(1, d), "ln_w": rwkv_ln_w[ia].reshape(1, d), "ln_b": rwkv_ln_b[ia].reshape(1, d),
                "w_o": rwkv_w_o[ia].astype(BF16),
            }
            r, v, kk, g, lw, kd, b = _rwkv_inputs(x2, mod, norm_mix[i].reshape(1, d), p, seg, segt, tps)
            yf, yr = _rwkv_scan(r, v, kk, lw, kd, b, batch, ctx_len)
            x2 = _rwkv_readout(yf, yr, r, kd, v, g, x2, mod, p, seg, segt, tps)
            ia += 1
        else:
            p = {
                "w_qkv": attn_w_qkv[ib].astype(BF16),
                "q_gain": jnp.tile(attn_q_gain[ib], nq).reshape(1, d),
                "k_gain": jnp.tile(attn_k_gain[ib], ATTN_KV_HEADS).reshape(1, ATTN_KV_HEADS * HEAD_DIM),
            }
            q, k, v = _gqa_project(x2, mod, norm_mix[i].reshape(1, d), p, rope, seg, segt, segk, segtk, tps)
            o = _attention(q, k, v, batch, tps, tps - 1, 1, tt)
            o = _attention(q, k, v, batch, tps, 1, 0, ctx_len, into=o)
            x2 = _attn_out(o, x2, mod, attn_w_o[ib].astype(BF16), tps)
            ib += 1
        f, afft = _moe_router(x2, mod, norm_ffn[i].reshape(1, d), router_pad[i], batch, tps)
        idx, gate = _moe_select(afft, sets, cap_l + cap_c)
        xe = _moe_gather(idx, f, batch)
        ye = _moe_experts(xe, moe_w_gate_up, moe_w_down, i)
        x2 = _moe_residual(x2, _moe_combine(idx, gate, ye, tt), mod, tps)
    return _final_norm(x2, final_norm.reshape(1, d), batch, tps)
```

```python
import functools

import jax
import jax.numpy as jnp
from jax import lax
from jax.experimental import pallas as pl
from jax.experimental.pallas import tpu as pltpu

F32 = jnp.float32
BF16 = jnp.bfloat16

HEAD_DIM = 64
N_MOD = 6
NORM_EPS = 1e-6
GN_EPS = 64e-5
ROPE_THETA = 10000.0
GRID_W = 64
ATTN_KV_HEADS = 4
N_EXPERTS = 16
EC_CAPACITY = 2
TM = 256
CHUNK = 64
LANES = 128
VMEM_LIMIT = 56 * 1024 * 1024
NEG_EXP_M05 = -0.6065306597126334
LOG2_E = 1.4426950408889634


def _cparams(*sem):
    return pltpu.CompilerParams(dimension_semantics=sem, vmem_limit_bytes=VMEM_LIMIT)


def _split2(x):
    hi = x.astype(BF16)
    lo = (x - hi.astype(F32)).astype(BF16)
    return hi, lo


def _split3(x):
    hi = x.astype(BF16)
    r1 = x - hi.astype(F32)
    mid = r1.astype(BF16)
    lo = (r1 - mid.astype(F32)).astype(BF16)
    return hi, mid, lo


def _dot(a, b):
    return jnp.dot(a, b, preferred_element_type=F32)


def _dot_nt(a, b):
    return lax.dot_general(a, b, (((1,), (1,)), ((), ())), preferred_element_type=F32)


def _dot_tn(a, b):
    return lax.dot_general(a, b, (((0,), (0,)), ((), ())), preferred_element_type=F32)


def _bdot(a, b):
    return _dot(a.astype(BF16), b.astype(BF16))


def _mm(a, b, passes, kind="nn"):
    f = {"nn": _dot, "nt": _dot_nt, "tn": _dot_tn}[kind]
    if passes == 1:
        return f(a.astype(BF16), b.astype(BF16))
    ah, al = _split2(a)
    bh, bl = _split2(b)
    return f(ah, bh) + (f(ah, bl) + f(al, bh))


def _dot_exact_rhs(a, b_bf16):
    h, l = _split2(a)
    return _dot(h, b_bf16) + _dot(l, b_bf16)


def _seg_sum(x, seg, segt):
    s = _dot_exact_rhs(x, seg)
    return _dot_exact_rhs(s, segt)


def _norm_mod(x, gain, shift, scale):
    ms = jnp.mean(x * x, axis=-1, keepdims=True)
    y = x * lax.rsqrt(ms + NORM_EPS)
    return (y * gain) * (1.0 + scale) + shift


def _sigmoid(x):
    return 0.5 * jnp.tanh(0.5 * x) + 0.5


def _silu(x):
    return x * _sigmoid(x)


def _mod_row(i, tiles_per_sample):
    return jnp.where(i % tiles_per_sample == 0, 0, 1 + i // tiles_per_sample)


def _const_spec(shape):
    nd = len(shape)
    return pl.BlockSpec(shape, lambda *_: (0,) * nd)


def _mod_kernel(c_ref, w_ref, b_ref, o_ref):
    s = _silu(c_ref[...])
    o_ref[0] = _mm(s, w_ref[0], 3) + b_ref[0]


def _mod_vectors(cc, mod_w, mod_b):
    depth, d, n = mod_w.shape
    tn = 1536
    return pl.pallas_call(
        _mod_kernel,
        out_shape=jax.ShapeDtypeStruct((depth, 16, n), F32),
        grid=(depth, n // tn),
        in_specs=[
            pl.BlockSpec((16, d), lambda l, j: (0, 0)),
            pl.BlockSpec((1, d, tn), lambda l, j: (l, 0, j)),
            pl.BlockSpec((1, 1, tn), lambda l, j: (l, 0, j)),
        ],
        out_specs=pl.BlockSpec((1, 16, tn), lambda l, j: (l, 0, j)),
        compiler_params=_cparams("arbitrary", "arbitrary"),
        name="mod_vectors",
    )(cc, mod_w, mod_b.reshape(depth, 1, n))


def _rwkv_in_kernel(x_ref, xp_ref, xn_ref, m_ref, gain_ref, mu_ref, wrkv_ref, w1_ref, w2_ref, w0_ref,
                    a1_ref, a2_ref, a0_ref, g1_ref, g2_ref, kk_ref_, ka_ref, seg_ref, segt_ref,
                    r_o, v_o, kk_o, g_o, lw_o, kd_o, b_o, *, tiles_per_sample):
    i = pl.program_id(0)
    j = i % tiles_per_sample
    m = m_ref[0]
    shift, scale = m[0:1], m[1:2]
    gain = gain_ref[...]
    h = _norm_mod(x_ref[...], gain, shift, scale)
    has_prev = (j >= 2).astype(F32)
    has_next = jnp.logical_and(j >= 1, j <= tiles_per_sample - 2).astype(F32)
    hp_row = _norm_mod(xp_ref[7:8, :], gain, shift, scale) * has_prev
    hn_row = _norm_mod(xn_ref[0:1, :], gain, shift, scale) * has_next
    row = lax.broadcasted_iota(jnp.int32, h.shape, 0)
    h_prev = jnp.where(row == 0, hp_row, pltpu.roll(h, 1, 0))
    h_next = jnp.where(row == TM - 1, hn_row, pltpu.roll(h, TM - 1, 0))
    xx = 0.5 * (h_prev + h_next) - h
    mu = mu_ref[...]

    def mix(n):
        return (h + xx * mu[n:n + 1]).astype(BF16)

    r = _dot(mix(0), wrkv_ref[0])
    k = _dot(mix(1), wrkv_ref[1])
    v = _dot(mix(2), wrkv_ref[2])
    tw = jnp.tanh(_dot(mix(3), w1_ref[...])).astype(BF16)
    ua = _dot(mix(4), a1_ref[...]).astype(BF16)
    g = _dot(_sigmoid(_dot(mix(5), g1_ref[...])).astype(BF16), g2_ref[...])
    kk = k * kk_ref_[...]
    n2 = _seg_sum(kk * kk, seg_ref[...], segt_ref[...])
    kk = kk / jnp.maximum(jnp.sqrt(n2), 1e-12)
    r_o[...] = r
    v_o[...] = v
    kk_o[...] = kk
    g_o[...] = g
    ka = ka_ref[...]
    for z in range(2):
        w_pre = w0_ref[z:z + 1, :] + _dot(tw, w2_ref[z])
        lw_o[z] = NEG_EXP_M05 * _sigmoid(w_pre)
        a =_sigmoid(a0_ref[z:z + 1, :] + _dot(ua, a2_ref[z]))
        kd_o[z] = k * (1.0 + (a - 1.0) * ka)
        b_o[z] = kk * a


def _rwkv_inputs(x2, mod, gain, p, seg, segt, tiles_per_sample):
    n, d = x2.shape
    nt = n // TM
    blk8 = TM // 8
    last8 = n // 8 - 1
    row_spec = pl.BlockSpec((TM, d), lambda i: (i, 0))
    dir_spec = pl.BlockSpec((2, TM, d), lambda i: (0, i, 0))
    tok = jax.ShapeDtypeStruct((n, d), F32)
    tok2 = jax.ShapeDtypeStruct((2, n, d), F32)
    return pl.pallas_call(
        functools.partial(_rwkv_in_kernel, tiles_per_sample=tiles_per_sample),
        out_shape=(tok, tok, tok, tok, tok2, tok2, tok2),
        grid=(nt,),
        in_specs=[
            row_spec,
            pl.BlockSpec((8, d), lambda i: (jnp.maximum(i * blk8 - 1, 0), 0)),
            pl.BlockSpec((8, d), lambda i: (jnp.minimum((i + 1) * blk8, last8), 0)),
            pl.BlockSpec((1, N_MOD, d), lambda i: (_mod_row(i, tiles_per_sample), 0, 0)),
            _const_spec((1, d)),
            _const_spec((6, d)),
            _const_spec((3, d, d)),
            _const_spec((d, LANES)),
            _const_spec((2, LANES, d)),
            _const_spec((2, d)),
            _const_spec((d, LANES)),
            _const_spec((2, LANES, d)),
            _const_spec((2, d)),
            _const_spec((d, LANES)),
            _const_spec((LANES, d)),
            _const_spec((1, d)),
            _const_spec((1, d)),
            _const_spec((d, LANES)),
            _const_spec((LANES, d)),
        ],
        out_specs=(row_spec, row_spec, row_spec, row_spec, dir_spec, dir_spec, dir_spec),
        compiler_params=_cparams("arbitrary"),
        name="rwkv_inputs",
    )(x2, x2, x2, mod, gain, p["mu"], p["w_rkv"], p["w1"], p["w2"], p["w0"], p["a1"], p["a2"], p["a0"],
      p["g1"], p["g2"], p["k_k"], p["k_a"], seg, segt)


def _scan_consts(reverse):
    sgn = -1 if reverse else 1
    n2 = 2 * CHUNK
    row = lax.broadcasted_iota(jnp.int32, (n2, n2), 0)
    col = lax.broadcasted_iota(jnp.int32, (n2, n2), 1)
    same = (row // CHUNK) == (col // CHUNK)
    dt = (row % CHUNK - col % CHUNK) * sgn
    rc = lax.broadcasted_iota(jnp.int32, (CHUNK, CHUNK), 0)
    cc = lax.broadcasted_iota(jnp.int32, (CHUNK, CHUNK), 1)
    return {
        "same": same,
        "strict": jnp.logical_and(same, dt > 0),
        "incl": jnp.logical_and(same, dt >= 0),
        "eye": (row == col).astype(F32),
        "tri": ((rc - cc) * sgn >= 0).astype(BF16),
        "head0": lax.broadcasted_iota(jnp.int32, (CHUNK, LANES), 1) < HEAD_DIM,
    }


def _scan_units(units, passes):
    n2 = 2 * CHUNK
    every = lambda f: [f(u) for u in units]

    def stack(u, x):
        return jnp.concatenate([jnp.where(u["c"]["head0"], x, 0.0), jnp.where(u["c"]["head0"], 0.0, x)], axis=0)

    def prep(u):
        cl = _dot_exact_rhs_left(u["c"]["tri"], u["lw"])
        tot = jnp.sum(u["lw"], axis=0, keepdims=True)
        e_ncl = jnp.exp(-cl)
        e_end = jnp.exp(tot - cl)
        u["q2"] = jnp.concatenate([stack(u, u["kk"] * jnp.exp(cl - u["lw"])), stack(u, u["r"] * jnp.exp(cl))], axis=0)
        u["k2"] = jnp.concatenate([stack(u, u["kd"] * e_ncl), stack(u, u["b"] * e_ncl)], axis=0)
        u["ket"] = jnp.concatenate([u["kd"] * e_end, -(u["b"] * e_end)], axis=0).T
        u["g_col"] = jnp.sum(jnp.where(u["c"]["eye"] > 0, jnp.exp(tot), 0.0), axis=1, keepdims=True)
        u["vs"] = stack(u, u["v"])

    every(prep)
    a_all = every(lambda u: _mm(u["q2"], u["k2"], passes["a"], "nt"))
    qm = every(lambda u: _mm(u["q2"], u["m0"], passes["qm"]))
    for u, a in zip(units, a_all):
        c = u["c"]
        u["l_kk"] = jnp.where(c["strict"], a[:n2, :n2], 0.0)
        u["l_rk"] = jnp.where(c["incl"], a[n2:, :n2], 0.0)
        u["l_rb"] = jnp.where(c["incl"], a[n2:, n2:], 0.0)
        u["pw"] = -jnp.where(c["strict"], a[:n2, n2:], 0.0)
        u["inv"] = c["eye"] + u["pw"]
    lv = every(lambda u: _mm(u["l_kk"], u["vs"], passes["lv"]))
    levels = CHUNK.bit_length() - 1
    sq = every(lambda u: _mm(u["pw"], u["pw"], passes["inv"]))
    for u, x in zip(units, sq):
        u["pw"] = x
    for _ in range(1, levels - 1):
        st = every(lambda u: _mm(jnp.concatenate([u["pw"], u["inv"]], axis=0), u["pw"], passes["inv"]))
        for u, x in zip(units, st):
            u["pw"] = x[:n2]
            u["inv"] = u["inv"] + x[n2:]
    last = every(lambda u: _mm(u["inv"], u["pw"], passes["inv"]))
    for u, x in zip(units, last):
        u["inv"] = u["inv"] + x
    us = [_mm(u["inv"], q[:n2] + t, passes["us"]) for u, q, t in zip(units, qm, lv)]
    ys = [q[n2:] + _mm(jnp.concatenate([u["l_rk"], -u["l_rb"]], axis=1), jnp.concatenate([u["vs"], s], axis=0), passes["ys"])
          for u, q, s in zip(units, qm, us)]
    new = [_mm(u["ket"], jnp.concatenate([u["v"], s[:CHUNK] + s[CHUNK:]], axis=0), passes["upd"]) for u, s in zip(units, us)]
    ms = [u["m0"] * u["g_col"] + jnp.where(u["c"]["same"], x, 0.0) for u, x in zip(units, new)]
    return [y[:CHUNK] + y[CHUNK:] for y in ys], ms


def _dot_exact_rhs_left(a_bf16, b):
    h, m, l = _split3(b)
    return _dot(a_bf16, h) + (_dot(a_bf16, m) + _dot(a_bf16, l))


SCAN_PASSES = {"a": 1, "qm": 1, "lv": 1, "inv": 1, "us": 1, "ys": 1, "upd": 1}


def _scan_kernel(rf_ref, vf_ref, kkf_ref, rr_ref, vr_ref, kkr_ref, lwf_ref, kdf_ref, bf_ref, lwr_ref, kdr_ref,
                 br_ref, yf_ref, yr_ref, m_ref, *, pairs):
    @pl.when(pl.program_id(2) == 0)
    def _():
        m_ref[...] = jnp.zeros_like(m_ref)

    dirs = (
        (_scan_consts(False), rf_ref, vf_ref, kkf_ref, lwf_ref, kdf_ref, bf_ref, yf_ref),
        (_scan_consts(True), rr_ref, vr_ref, kkr_ref, lwr_ref, kdr_ref, br_ref, yr_ref),
    )
    units = []
    for z, (consts, r_ref, v_ref, kk_ref, lw_ref, kd_ref, b_ref, _) in enumerate(dirs):
        for p in range(pairs):
            sl = slice(p * LANES, (p + 1) * LANES)
            units.append({"c": consts, "r": r_ref[:, sl], "v": v_ref[:, sl], "kk": kk_ref[:, sl], "lw": lw_ref[0, :, sl],
                          "kd": kd_ref[0, :, sl], "b": b_ref[0, :, sl], "m0": m_ref[z * pairs + p]})
    ys, ms = _scan_units(units, SCAN_PASSES)
    for i, (y, m1) in enumerate(zip(ys, ms)):
        z, p = divmod(i, pairs)
        dirs[z][-1][0, :, p * LANES:(p + 1) * LANES] = y
        m_ref[i] = m1


def _rwkv_scan(r, v, kk, lw, kd, b, batch, ctx_len, pairs=8):
    n, d = r.shape
    tt = n // batch
    nch = tt // CHUNK
    nch_ctx = ctx_len // CHUNK
    width = pairs * LANES
    groups = d // width

    fwd_row = lambda bb, c: bb * nch + c
    rev_row = lambda bb, c: bb * nch + jnp.where(c < nch_ctx, nch_ctx - 1 - c, nch + nch_ctx - 1 - c)
    shared = lambda row: pl.BlockSpec((CHUNK, width), lambda bb, g, c: (row(bb, c), g))
    per_dir = lambda z, row: pl.BlockSpec((1, CHUNK, width), lambda bb, g, c: (z, row(bb, c), g))
    return pl.pallas_call(
        functools.partial(_scan_kernel, pairs=pairs),
        out_shape=(jax.ShapeDtypeStruct((1, n, d), F32), jax.ShapeDtypeStruct((1, n, d), F32)),
        grid=(batch, groups, nch),
        in_specs=[shared(fwd_row)] * 3 + [shared(rev_row)] * 3 + [per_dir(0, fwd_row)] * 3 + [per_dir(1, rev_row)] * 3,
        out_specs=(per_dir(0, fwd_row), per_dir(0, rev_row)),
        scratch_shapes=[pltpu.VMEM((2 * pairs, LANES, LANES), F32)],
        compiler_params=_cparams("arbitrary", "arbitrary", "arbitrary"),
        name="rwkv_scan",
    )(r, v, kk, r, v, kk, lw, kd, b, lw, kd, b)


def _rwkv_out_kernel(y0_ref, y1_ref, r_ref, kd0_ref, kd1_ref, v_ref, g_ref, x_ref, m_ref, rk_ref, lnw_ref,
                     lnb_ref, wo_ref, seg_ref, segt_ref, o_ref):
    seg, segt = seg_ref[...], segt_ref[...]
    y = y0_ref[0] + y1_ref[0]
    mean = _seg_sum(y, seg, segt) * (1.0 / HEAD_DIM)
    dy = y - mean
    var = _seg_sum(dy * dy, seg, segt) * (1.0 / HEAD_DIM)
    yn = (dy * lax.rsqrt(var + GN_EPS)) * lnw_ref[...] + lnb_ref[...]
    bonus = _seg_sum(r_ref[...] * (kd0_ref[0] + kd1_ref[0]) * rk_ref[...], seg, segt)
    out = (yn + bonus * v_ref[...]) * g_ref[...]
    o = _dot(out.astype(BF16), wo_ref[...])
    o_ref[...] = x_ref[...] + m_ref[0][2:3] * o


def _rwkv_readout(yf, yr, r, kd, v, g, x2, mod, p, seg, segt, tiles_per_sample):
    n, d = x2.shape
    row_spec = pl.BlockSpec((TM, d), lambda i: (i, 0))
    return pl.pallas_call(
        _rwkv_out_kernel,
        out_shape=jax.ShapeDtypeStruct((n, d), F32),
        grid=(n // TM,),
        in_specs=[
            pl.BlockSpec((1, TM, d), lambda i: (0, i, 0)),
            pl.BlockSpec((1, TM, d), lambda i: (0, i, 0)),
            row_spec,
            pl.BlockSpec((1, TM, d), lambda i: (0, i, 0)),
            pl.BlockSpec((1, TM, d), lambda i: (1, i, 0)),
            row_spec, row_spec, row_spec,
            pl.BlockSpec((1, N_MOD, d), lambda i: (_mod_row(i, tiles_per_sample), 0, 0)),
            _const_spec((1, d)), _const_spec((1, d)), _const_spec((1, d)),
            _const_spec((d, d)),
            _const_spec((d, LANES)), _const_spec((LANES, d)),
        ],
        out_specs=row_spec,
        compiler_params=_cparams("arbitrary"),
        name="rwkv_readout",
    )(yf, yr, r, kd, kd, v, g, x2, mod, p["r_k"], p["ln_w"], p["ln_b"], p["w_o"], seg, segt)


def _rope(x, cos, sin_lo, sin_hi):
    w = x.shape[1]
    reps = w // LANES
    tile = lambda t: jnp.concatenate([t] * reps, axis=1) if reps > 1 else t
    half = HEAD_DIM // 4
    return x * tile(cos) + pltpu.roll(x, w - half, 1) * tile(sin_lo) + pltpu.roll(x, half, 1) * tile(sin_hi)


def _gqa_proj_kernel(x_ref, m_ref, gain_ref, w_ref, qg_ref, kg_ref, cos_ref, slo_ref, shi_ref, seg_ref, segt_ref,
                     segk_ref, segtk_ref, q_o, k_o, v_o, *, d, dkv):
    m = m_ref[0]
    h = _norm_mod(x_ref[...], gain_ref[...], m[0:1], m[1:2]).astype(BF16)
    qkv = _dot(h, w_ref[...])
    q, k, v = qkv[:, :d], qkv[:, d:d + dkv], qkv[:, d + dkv:]
    cos, slo, shi = cos_ref[...], slo_ref[...], shi_ref[...]
    qms = _seg_sum(q * q, seg_ref[...], segt_ref[...]) * (1.0 / HEAD_DIM)
    q = (q * lax.rsqrt(qms + NORM_EPS)) * qg_ref[...]
    kms = _seg_sum(k * k, segk_ref[...], segtk_ref[...]) * (1.0 / HEAD_DIM)
    k = (k * lax.rsqrt(kms + NORM_EPS)) * kg_ref[...]
    q = (_rope(q, cos, slo, shi) * (HEAD_DIM ** -0.5 * LOG2_E)).astype(BF16)
    k = _rope(k, cos, slo, shi).astype(BF16)
    v = v.astype(BF16)
    lane = lax.broadcasted_iota(jnp.int32, (v.shape[0], LANES - HEAD_DIM), 1)
    ones_pad = jnp.where(lane == 0, 1.0, 0.0).astype(BF16)
    for hh in range(d // HEAD_DIM):
        q_o[hh] = q[:, hh * HEAD_DIM:(hh + 1) * HEAD_DIM]
    for hh in range(dkv // HEAD_DIM):
        k_o[hh] = k[:, hh * HEAD_DIM:(hh + 1) * HEAD_DIM]
        v_o[hh] = jnp.concatenate([v[:, hh * HEAD_DIM:(hh + 1) * HEAD_DIM], ones_pad], axis=1)


def _gqa_project(x2, mod, gain, p, rope, seg, segt, segk, segtk, tiles_per_sample):
    n, d = x2.shape
    dkv = ATTN_KV_HEADS * HEAD_DIM
    nq, nkv = d // HEAD_DIM, ATTN_KV_HEADS
    heads = lambda h: pl.BlockSpec((h, TM, HEAD_DIM), lambda i: (0, i, 0))
    pos = pl.BlockSpec((TM, LANES), lambda i: (i % tiles_per_sample, 0))
    return pl.pallas_call(
        functools.partial(_gqa_proj_kernel, d=d, dkv=dkv),
        out_shape=(jax.ShapeDtypeStruct((nq, n, HEAD_DIM), BF16), jax.ShapeDtypeStruct((nkv, n, HEAD_DIM), BF16),
                   jax.ShapeDtypeStruct((nkv, n, LANES), BF16)),
        grid=(n // TM,),
        in_specs=[
            pl.BlockSpec((TM, d), lambda i: (i, 0)),
            pl.BlockSpec((1, N_MOD, d), lambda i: (_mod_row(i, tiles_per_sample), 0, 0)),
            _const_spec((1, d)),
            _const_spec((d, d + 2 * dkv)),
            _const_spec((1, d)), _const_spec((1, dkv)),
            pos, pos, pos,
            _const_spec((d, LANES)), _const_spec((LANES, d)),
            _const_spec((dkv, LANES)), _const_spec((LANES, dkv)),
        ],
        out_specs=(heads(nq), heads(nkv), pl.BlockSpec((nkv, TM, LANES), lambda i: (0, i, 0))),
        compiler_params=_cparams("arbitrary"),
        name="gqa_project",
    )(x2, mod, gain, p["w_qkv"], p["q_gain"], p["k_gain"], rope[0], rope[1], rope[2], seg, segt, segk, segtk)


def _attn_kernel(q_ref, k_ref, v_ref, o_ref, *, group, kvs, ctx_rows):
    def attend(kv_rows):
        for kv in range(kvs):
            k = k_ref[kv, :kv_rows, :]
            v = v_ref[kv, :kv_rows, :]
            for hh in range(kv * group, (kv + 1) * group):
                s = _dot_nt(q_ref[hh], k)
                pr = jnp.exp2(s - jnp.max(s, axis=-1, keepdims=True))
                oe = _dot(pr.astype(BF16), v)
                o_ref[hh] = (oe[:, :HEAD_DIM] / oe[:, HEAD_DIM:HEAD_DIM + 1]).astype(BF16)

    is_ctx = pl.program_id(2) == 0
    pl.when(is_ctx)(lambda: attend(ctx_rows))
    pl.when(jnp.logical_not(is_ctx))(lambda: attend(k_ref.shape[1]))


def _attention(q, k, v, batch, tiles_per_sample, ctx_rows, kvs=2):
    nq, n, _ = q.shape
    nkv = k.shape[0]
    group = nq // nkv
    tt = n // batch
    q_spec = pl.BlockSpec((kvs * group, TM, HEAD_DIM), lambda bb, g, t: (g, bb * tiles_per_sample + t, 0))
    return pl.pallas_call(
        functools.partial(_attn_kernel, group=group, kvs=kvs, ctx_rows=ctx_rows),
        out_shape=jax.ShapeDtypeStruct((nq, n, HEAD_DIM), BF16),
        grid=(batch, nkv // kvs, tiles_per_sample),
        in_specs=[q_spec, pl.BlockSpec((kvs, tt, HEAD_DIM), lambda bb, g, t: (g, bb, 0)),
                  pl.BlockSpec((kvs, tt, LANES), lambda bb, g, t: (g, bb, 0))],
        out_specs=q_spec,
        compiler_params=_cparams("arbitrary", "arbitrary", "arbitrary"),
        name="gqa_attention",
    )(q, k, v)


def _attn_out_kernel(o_ref, x_ref, m_ref, wo_ref, y_ref):
    o = jnp.concatenate([o_ref[hh] for hh in range(o_ref.shape[0])], axis=1)
    y_ref[...] = x_ref[...] + m_ref[0][2:3] * _dot(o, wo_ref[...])


def _attn_out(o, x2, mod, w_o, tiles_per_sample):
    n, d = x2.shape
    nq = o.shape[0]
    row_spec = pl.BlockSpec((TM, d), lambda i: (i, 0))
    return pl.pallas_call(
        _attn_out_kernel,
        out_shape=jax.ShapeDtypeStruct((n, d), F32),
        grid=(n // TM,),
        in_specs=[
            pl.BlockSpec((nq, TM, HEAD_DIM), lambda i: (0, i, 0)),
            row_spec,
            pl.BlockSpec((1, N_MOD, d), lambda i: (_mod_row(i, tiles_per_sample), 0, 0)),
            _const_spec((d, d)),
        ],
        out_specs=row_spec,
        compiler_params=_cparams("arbitrary"),
        name="gqa_out_proj",
    )(o, x2, mod, w_o)


def _router_kernel(x_ref, m_ref, gain_ref, wr_ref, f_o, afft_o):
    m = m_ref[0]
    f = _norm_mod(x_ref[...], gain_ref[...], m[3:4], m[4:5])
    f_o[...] = f
    logits = _mm(f, wr_ref[...], 3)
    lane = lax.broadcasted_iota(jnp.int32, logits.shape, 1)
    logits = jnp.where(lane < N_EXPERTS, logits, -1e30)
    e = jnp.exp(logits - jnp.max(logits, axis=-1, keepdims=True))
    aff = e / jnp.sum(e, axis=-1, keepdims=True)
    afft_o[0] = aff.T[:N_EXPERTS]


def _moe_router(x2, mod, gain, w_router, batch, tiles_per_sample):
    n, d = x2.shape
    tt = n // batch
    return pl.pallas_call(
        _router_kernel,
        out_shape=(jax.ShapeDtypeStruct((n, d), F32), jax.ShapeDtypeStruct((batch, N_EXPERTS, tt), F32)),
        grid=(n // TM,),
        in_specs=[
            pl.BlockSpec((TM, d), lambda i: (i, 0)),
            pl.BlockSpec((1, N_MOD, d), lambda i: (_mod_row(i, tiles_per_sample), 0, 0)),
            _const_spec((1, d)),
            _const_spec((d, LANES)),
        ],
        out_specs=(pl.BlockSpec((TM, d), lambda i: (i, 0)),
                   pl.BlockSpec((1, N_EXPERTS, TM), lambda i: (i // tiles_per_sample, 0, i % tiles_per_sample))),
        compiler_params=_cparams("arbitrary"),
        name="moe_router",
    )(x2, mod, gain, w_router)


def _prefix_excl(mask, tri_excl):
    xb = mask.astype(BF16)
    carry = jnp.zeros((mask.shape[0], 1), F32)
    outs = []
    for blk in range(mask.shape[1] // LANES):
        piece = xb[:, blk * LANES:(blk + 1) * LANES]
        outs.append(_dot(piece, tri_excl) + carry)
        carry = carry + jnp.sum(piece.astype(F32), axis=1, keepdims=True)
    return jnp.concatenate(outs, axis=1) if len(outs) > 1 else outs[0]


def _top_cap(a, cap, tri_excl):
    ai = lax.bitcast_convert_type(a, jnp.int32)

    def body(i, thr):
        cand = thr | jnp.left_shift(jnp.int32(1), 30 - i)
        cnt = jnp.sum((ai >= cand).astype(F32), axis=1, keepdims=True)
        return jnp.where(cnt >= cap, cand, thr)

    thr = lax.fori_loop(0, 31, body, jnp.zeros((a.shape[0], 1), jnp.int32))
    gt = ai > thr
    eq = ai == thr
    need = cap - jnp.sum(gt.astype(F32), axis=1, keepdims=True)
    sel = jnp.logical_or(gt, jnp.logical_and(eq, _prefix_excl(eq, tri_excl) < need))
    return sel, _prefix_excl(sel, tri_excl)


def _select_kernel(afft_ref, tri_ref, idx_o, gate_o, sel_s, pos_s, *, sets):
    e = pl.program_id(1)

    @pl.when(e == 0)
    def _():
        for off, n, cap, _ in sets:
            sel, pos = _top_cap(afft_ref[0, :, off:off + n], cap, tri_ref[...])
            sel_s[:, off:off + n] = sel.astype(F32)
            pos_s[:, off:off + n] = pos

    for off, n, cap, slot0 in sets:
        a_e = afft_ref[0, pl.ds(e, 1), off:off + n]
        sel_e = sel_s[pl.ds(e, 1), off:off + n]
        pos_e = pos_s[pl.ds(e, 1), off:off + n]
        wl = min(cap, LANES)
        nw = cap // wl
        pos_i = pos_e.astype(jnp.int32)
        lane_of = jnp.bitwise_and(pos_i, wl - 1)
        win_of = jnp.right_shift(pos_i, wl.bit_length() - 1)
        lane_id = lax.broadcasted_iota(jnp.int32, (wl, n), 0)
        onehot = jnp.where(jnp.logical_and(lane_of == lane_id, sel_e > 0), 1.0, 0.0).astype(BF16)
        tok = lax.broadcasted_iota(jnp.int32, (16, n), 1)
        rid = lax.broadcasted_iota(jnp.int32, (16, n), 0)
        a_h, a_m, a_l = _split3(a_e)
        rows = jnp.where(rid == 0, jnp.right_shift(tok, 6).astype(F32),
               jnp.where(rid == 1, jnp.bitwise_and(tok, 63).astype(F32),
               jnp.where(rid == 2, a_h.astype(F32),
               jnp.where(rid == 3, a_m.astype(F32),
               jnp.where(rid == 4, a_l.astype(F32), 0.0)))))
        stacked = jnp.concatenate([jnp.where(win_of == wi, rows, 0.0) for wi in range(nw)], axis=0).astype(BF16)
        res_all = _dot_nt(stacked, onehot)
        for wi in range(nw):
            res = res_all[16 * wi:16 * (wi + 1)]
            lo = slot0 + wi * wl
            idx_o[0, 0, :, lo:lo + wl] = (res[0:1] * 64.0 + res[1:2]).astype(jnp.int32) + off
            gate_o[0, 0, :, lo:lo + wl] = res[2:3] + (res[3:4] + res[4:5])


def _moe_select(afft, sets, n_slots):
    batch, ne, tt = afft.shape
    tri = (jnp.arange(LANES)[:, None] < jnp.arange(LANES)[None, :]).astype(BF16)
    slot_spec = pl.BlockSpec((1, 1, 1, n_slots), lambda bb, e: (bb, e, 0, 0))
    return pl.pallas_call(
        functools.partial(_select_kernel, sets=sets),
        out_shape=(jax.ShapeDtypeStruct((batch, ne, 1, n_slots), jnp.int32),
                   jax.ShapeDtypeStruct((batch, ne, 1, n_slots), F32)),
        grid=(batch, ne),
        in_specs=[pl.BlockSpec((1, ne, tt), lambda bb, e: (bb, 0, 0)), _const_spec((LANES, LANES))],
        out_specs=(slot_spec, slot_spec),
        scratch_shapes=[pltpu.VMEM((ne, tt), F32), pltpu.VMEM((ne, tt), F32)],
        compiler_params=_cparams("arbitrary", "arbitrary"),
        name="moe_select",
    )(afft, tri)


def _gather_kernel(idx_ref, f_ref, xe_o, buf, *, n_slots):
    def body(c, _):
        r = idx_ref[0, 0, 0, c]
        buf[pl.ds(c, 1), :] = f_ref[pl.ds(r, 1), :]
        return 0

    lax.fori_loop(0, n_slots, body, 0, unroll=8)
    xe_o[0, 0] = buf[...].astype(BF16)


def _moe_gather(idx, f, batch):
    n, d = f.shape
    tt = n // batch
    ne, n_slots = idx.shape[1], idx.shape[3]
    return pl.pallas_call(
        functools.partial(_gather_kernel, n_slots=n_slots),
        out_shape=jax.ShapeDtypeStruct((batch, ne, n_slots, d), BF16),
        grid=(batch, ne),
        in_specs=[
            pl.BlockSpec((1, 1, 1, n_slots), lambda bb, e: (bb, e, 0, 0), memory_space=pltpu.SMEM),
            pl.BlockSpec((tt, d), lambda bb, e: (bb, 0)),
        ],
        out_specs=pl.BlockSpec((1, 1, n_slots, d), lambda bb, e: (bb, e, 0, 0)),
        scratch_shapes=[pltpu.VMEM((n_slots, d), F32)],
        compiler_params=_cparams("arbitrary", "arbitrary"),
        name="moe_gather",
    )(idx, f)


def _expert_kernel(xe_ref, wg_ref, wu_ref, wd_ref, ml_ref, mc_ref, ye_o, *, cap_l):
    first = pl.program_id(2) == 0
    last = pl.program_id(2) == pl.num_programs(2) - 1
    wg = wg_ref[0, 0].astype(BF16)
    wu = wu_ref[0, 0].astype(BF16)
    wd = wd_ref[0, 0].astype(BF16)
    for i in range(xe_ref.shape[0]):
        x = xe_ref[i, 0]
        h = (_silu(_dot(x, wg)) * _dot(x, wu)).astype(BF16)
        y = _dot(h, wd)
        ye_o[i, 0] = jnp.where(first, y, ye_o[i, 0] + y)

    @pl.when(last)
    def _():
        latent_slot = lax.broadcasted_iota(jnp.int32, ye_o.shape[2:], 0) < cap_l
        for i in range(xe_ref.shape[0]):
            ye_o[i, 0] = ye_o[i, 0] * jnp.where(latent_slot, ml_ref[i], mc_ref[...])


def _moe_experts(xe, w_gate_up, w_down, gate_lat, gate_ctx, layer, cap_l, halves=2, tf=256):
    batch, ne, s, d = xe.shape
    de = w_down.shape[2]
    nf = de // tf
    bh = batch // halves
    tok = pl.BlockSpec((bh, 1, s, d), lambda e, mh, f: (mh, e, 0, 0))
    return pl.pallas_call(
        functools.partial(_expert_kernel, cap_l=cap_l),
        out_shape=jax.ShapeDtypeStruct((batch, ne, s, d), F32),
        grid=(ne, halves, nf),
        in_specs=[
            tok,
            pl.BlockSpec((1, 1, d, tf), lambda e, mh, f: (layer, e, 0, f)),
            pl.BlockSpec((1, 1, d, tf), lambda e, mh, f: (layer, e, 0, nf + f)),
            pl.BlockSpec((1, 1, tf, d), lambda e, mh, f: (layer, e, f, 0)),
            pl.BlockSpec((bh, 1, d), lambda e, mh, f: (mh, 0, 0)),
            _const_spec((1, d)),
        ],
        out_specs=tok,
        compiler_params=_cparams("arbitrary", "arbitrary", "arbitrary"),
        name="moe_experts",
    )(xe, w_gate_up, w_gate_up, w_down, gate_lat, gate_ctx)


def _combine_kernel(idx_ref, gate_ref, ye_ref, x_hbm, out_o, sem, *, n_slots):
    bb = pl.program_id(0)
    rows = out_o.shape[0]

    @pl.when(pl.program_id(1) == 0)
    def _():
        seed = pltpu.make_async_copy(x_hbm.at[pl.ds(bb * rows, rows)], out_o, sem)
        seed.start()
        seed.wait()

    def body(c, _):
        r = idx_ref[0, 0, 0, c]
        out_o[pl.ds(r, 1), :] += gate_ref[0, 0, 0, c] * ye_ref[0, 0, pl.ds(c, 1), :]
        return 0

    lax.fori_loop(0, n_slots, body, 0, unroll=8)


def _moe_combine(idx, gate, ye, x2, tt):
    batch, ne, n_slots, d = ye.shape
    smem = pl.BlockSpec((1, 1, 1, n_slots), lambda bb, e: (bb, e, 0, 0), memory_space=pltpu.SMEM)
    return pl.pallas_call(
        functools.partial(_combine_kernel, n_slots=n_slots),
        out_shape=jax.ShapeDtypeStruct((batch * tt, d), F32),
        grid=(batch, ne),
        in_specs=[smem, smem, pl.BlockSpec((1, 1, n_slots, d), lambda bb, e: (bb, e, 0, 0)),
                  pl.BlockSpec(memory_space=pl.ANY)],
        out_specs=pl.BlockSpec((tt, d), lambda bb, e: (bb, 0)),
        scratch_shapes=[pltpu.SemaphoreType.DMA(())],
        compiler_params=_cparams("arbitrary", "arbitrary"),
        name="moe_combine",
    )(idx, gate, ye, x2)


def _final_kernel(x_ref, g_ref, o_ref):
    x = x_ref[...]
    ms = jnp.mean(x * x, axis=-1, keepdims=True)
    o_ref[0] = (x * lax.rsqrt(ms + NORM_EPS)) * g_ref[...]


def _final_norm(x2, gain, batch, tiles_per_sample):
    n, d = x2.shape
    lat_tiles = tiles_per_sample - 1
    return pl.pallas_call(
        _final_kernel,
        out_shape=jax.ShapeDtypeStruct((batch, lat_tiles * TM, d), F32),
        grid=(batch, lat_tiles),
        in_specs=[pl.BlockSpec((TM, d), lambda bb, t: (bb * tiles_per_sample + 1 + t, 0)), _const_spec((1, d))],
        out_specs=pl.BlockSpec((1, TM, d), lambda bb, t: (bb, t, 0)),
        compiler_params=_cparams("arbitrary", "arbitrary"),
        name="final_norm",
    )(x2, gain)


def _segment_matrices(width):
    heads = jnp.arange(width) // HEAD_DIM
    seg = (heads[:, None] == jnp.arange(LANES)[None, :]).astype(BF16)
    return seg, seg.T


def _rope_tables(seq, ctx_len):
    t = jnp.arange(seq)
    pos = jnp.stack([(t // GRID_W).astype(F32), (t % GRID_W).astype(F32)], axis=-1)
    half = HEAD_DIM // 4
    inv_freq = ROPE_THETA ** (-jnp.arange(0, 2 * half, 2, dtype=F32) / (2 * half))
    ang = pos[:, :, None] * inv_freq
    cos, sin = jnp.cos(ang), jnp.sin(ang)
    zero = jnp.zeros_like(sin)
    per_head = lambda first, second: jnp.concatenate([first, second], axis=-1).reshape(seq, HEAD_DIM)
    cos_h = per_head(cos, cos)
    sin_lo = per_head(-sin, zero)
    sin_hi = per_head(zero, sin)
    pad = lambda tab, fill: jnp.concatenate([jnp.full((ctx_len, HEAD_DIM), fill, F32), tab], axis=0)
    two = lambda tab: jnp.concatenate([tab, tab], axis=1)
    return two(pad(cos_h, 1.0)), two(pad(sin_lo, 0.0)), two(pad(sin_hi, 0.0))


def kernel(x, c, ctx, c_ctx, mod_w, mod_b, norm_mix, norm_ffn, rwkv_mu, rwkv_w_rkv, rwkv_w0, rwkv_w1, rwkv_w2,
           rwkv_a0, rwkv_a1, rwkv_a2, rwkv_g1, rwkv_g2, rwkv_k_k, rwkv_k_a, rwkv_r_k, rwkv_ln_w, rwkv_ln_b,
           rwkv_w_o, attn_w_qkv, attn_q_gain, attn_k_gain, attn_w_o, moe_router, moe_w_gate_up, moe_w_down,
           final_norm):
    batch, seq, d = x.shape
    ctx_len = ctx.shape[1]
    depth = mod_w.shape[0]
    tt = ctx_len + seq
    tps = tt // TM
    assert ctx_len == TM and seq % TM == 0 and CHUNK == HEAD_DIM and batch + 1 <= 16
    nq = d // HEAD_DIM

    x2 = jnp.concatenate([ctx, x], axis=1).reshape(batch * tt, d)
    cc = jnp.zeros((16, d), F32).at[0].set(c_ctx).at[1:batch + 1].set(c)
    mods = _mod_vectors(cc, mod_w, mod_b).reshape(depth, 16, N_MOD, d)
    seg, segt = _segment_matrices(d)
    segk, segtk = _segment_matrices(ATTN_KV_HEADS * HEAD_DIM)
    rope = _rope_tables(seq, ctx_len)
    cap_l = EC_CAPACITY * seq // N_EXPERTS
    cap_c = EC_CAPACITY * ctx_len // N_EXPERTS
    sets = ((ctx_len, seq, cap_l, 0), (0, ctx_len, cap_c, cap_l))
    router_pad = jnp.zeros((depth, d, LANES), F32).at[:, :, :N_EXPERTS].set(moe_router)

    ia = ib = 0
    for i in range(depth):
        mod = mods[i]
        if i % 2 == 0:
            zpad = jnp.zeros((HEAD_DIM, d), F32)
            lora_pad = lambda w: jnp.stack([jnp.concatenate([w[0], zpad], 0), jnp.concatenate([zpad, w[1]], 0)]).astype(BF16)
            p = {
                "mu": rwkv_mu[ia], "w_rkv": rwkv_w_rkv[ia].astype(BF16),
                "w1": jnp.concatenate([rwkv_w1[ia, 0], rwkv_w1[ia, 1]], axis=1).astype(BF16), "w2": lora_pad(rwkv_w2[ia]),
                "w0": rwkv_w0[ia],
                "a1": jnp.concatenate([rwkv_a1[ia, 0], rwkv_a1[ia, 1]], axis=1).astype(BF16), "a2": lora_pad(rwkv_a2[ia]),
                "a0": rwkv_a0[ia],
                "g1": rwkv_g1[ia].astype(BF16), "g2": rwkv_g2[ia].astype(BF16),
                "k_k": rwkv_k_k[ia].reshape(1, d), "k_a": rwkv_k_a[ia].reshape(1, d),
                "r_k": rwkv_r_k[ia].reshape(1, d), "ln_w": rwkv_ln_w[ia].reshape(1, d), "ln_b": rwkv_ln_b[ia].reshape(1, d),
                "w_o": rwkv_w_o[ia].astype(BF16),
            }
            r, v, kk, g, lw, kd, b = _rwkv_inputs(x2, mod, norm_mix[i].reshape(1, d), p, seg, segt, tps)
            yf, yr = _rwkv_scan(r, v, kk, lw, kd, b, batch, ctx_len)
            x2 = _rwkv_readout(yf, yr, r, kd, v, g, x2, mod, p, seg, segt, tps)
            ia += 1
        else:
            p = {
                "w_qkv": attn_w_qkv[ib].astype(BF16),
                "q_gain": jnp.tile(attn_q_gain[ib], nq).reshape(1, d),
                "k_gain": jnp.tile(attn_k_gain[ib], ATTN_KV_HEADS).reshape(1, ATTN_KV_HEADS * HEAD_DIM),
            }
            q, k, v = _gqa_project(x2, mod, norm_mix[i].reshape(1, d), p, rope, seg, segt, segk, segtk, tps)
            o = _attention(q, k, v, batch, tps, ctx_len)
            x2 = _attn_out(o, x2, mod, attn_w_o[ib].astype(BF16), tps)
            ib += 1
        f, afft = _moe_router(x2, mod, norm_ffn[i].reshape(1, d), router_pad[i], batch, tps)
        idx, gate = _moe_select(afft, sets, cap_l + cap_c)
        xe = _moe_gather(idx, f, batch)
        ye = _moe_experts(xe, moe_w_gate_up, moe_w_down, mod[1:batch + 1, 5:6], mod[0, 5:6], i, cap_l)
        x2 = _moe_combine(idx, gate, ye, x2, tt)
    return _final_norm(x2, final_norm.reshape(1, d), batch, tps)
```

```python
import functools

import jax
import jax.numpy as jnp
from jax import lax
from jax.experimental import pallas as pl
from jax.experimental.pallas import tpu as pltpu

F32 = jnp.float32
BF16 = jnp.bfloat16

HEAD_DIM = 64
N_MOD = 6
NORM_EPS = 1e-6
GN_EPS = 64e-5
ROPE_THETA = 10000.0
GRID_W = 64
ATTN_KV_HEADS = 4
N_EXPERTS = 16
EC_CAPACITY = 2
TM = 256
CHUNK = 64
LANES = 128
VMEM_LIMIT = 56 * 1024 * 1024
NEG_EXP_M05 = -0.6065306597126334
LOG2_E = 1.4426950408889634
F32_VALUE_BITS = 31
BF16_ROWS = 16
TOK_RADIX = 64


def _cparams(*sem):
    return pltpu.CompilerParams(dimension_semantics=sem, vmem_limit_bytes=VMEM_LIMIT)


def _split2(x):
    hi = x.astype(BF16)
    lo = (x - hi.astype(F32)).astype(BF16)
    return hi, lo


def _split3(x):
    hi = x.astype(BF16)
    r1 = x - hi.astype(F32)
    mid = r1.astype(BF16)
    lo = (r1 - mid.astype(F32)).astype(BF16)
    return hi, mid, lo


def _dot(a, b):
    return jnp.dot(a, b, preferred_element_type=F32)


def _dot_nt(a, b):
    return lax.dot_general(a, b, (((1,), (1,)), ((), ())), preferred_element_type=F32)


def _mm(a, b, nt=False):
    return (_dot_nt if nt else _dot)(a.astype(BF16), b.astype(BF16))


def _mm3(a, b):
    ah, al = _split2(a)
    bh, bl = _split2(b)
    return _dot(ah, bh) + (_dot(ah, bl) + _dot(al, bh))


def _dot_exact_rhs(a, b_bf16):
    h, l = _split2(a)
    return _dot(h, b_bf16) + _dot(l, b_bf16)


def _seg_sum(x, seg, segt):
    s = _dot_exact_rhs(x, seg)
    return _dot_exact_rhs(s, segt)


def _norm_mod(x, gain, shift, scale):
    ms = jnp.mean(x * x, axis=-1, keepdims=True)
    y = x * lax.rsqrt(ms + NORM_EPS)
    return (y * gain) * (1.0 + scale) + shift


def _sigmoid(x):
    return 0.5 * jnp.tanh(0.5 * x) + 0.5


def _silu(x):
    return x * _sigmoid(x)


def _mod_row(i, tiles_per_sample):
    return jnp.where(i % tiles_per_sample == 0, 0, 1 + i // tiles_per_sample)


def _const_spec(shape):
    nd = len(shape)
    return pl.BlockSpec(shape, lambda *_: (0,) * nd)


def _mod_kernel(c_ref, w_ref, b_ref, o_ref):
    s = _silu(c_ref[...])
    o_ref[0] = _mm3(s, w_ref[0]) + b_ref[0]


def _mod_vectors(cc, mod_w, mod_b):
    depth, d, n = mod_w.shape
    tn = 1536
    return pl.pallas_call(
        _mod_kernel,
        out_shape=jax.ShapeDtypeStruct((depth, 16, n), F32),
        grid=(depth, n // tn),
        in_specs=[
            pl.BlockSpec((16, d), lambda l, j: (0, 0)),
            pl.BlockSpec((1, d, tn), lambda l, j: (l, 0, j)),
            pl.BlockSpec((1, 1, tn), lambda l, j: (l, 0, j)),
        ],
        out_specs=pl.BlockSpec((1, 16, tn), lambda l, j: (l, 0, j)),
        compiler_params=_cparams("arbitrary", "arbitrary"),
        name="mod_vectors",
    )(cc, mod_w, mod_b.reshape(depth, 1, n))


def _rwkv_in_kernel(x_ref, xp_ref, xn_ref, m_ref, gain_ref, mu_ref, wrkv_ref, w1_ref, w2_ref, w0_ref,
                    a1_ref, a2_ref, a0_ref, g1_ref, g2_ref, kk_ref_, ka_ref, seg_ref, segt_ref,
                    r_o, v_o, kk_o, g_o, lw_o, kd_o, b_o, *, tiles_per_sample):
    i = pl.program_id(0)
    j = i % tiles_per_sample
    m = m_ref[0]
    shift, scale = m[0:1], m[1:2]
    gain = gain_ref[...]
    h = _norm_mod(x_ref[...], gain, shift, scale)
    has_prev = (j >= 2).astype(F32)
    has_next = jnp.logical_and(j >= 1, j <= tiles_per_sample - 2).astype(F32)
    hp_row = _norm_mod(xp_ref[7:8, :], gain, shift, scale) * has_prev
    hn_row = _norm_mod(xn_ref[0:1, :], gain, shift, scale) * has_next
    row = lax.broadcasted_iota(jnp.int32, h.shape, 0)
    h_prev = jnp.where(row == 0, hp_row, pltpu.roll(h, 1, 0))
    h_next = jnp.where(row == TM - 1, hn_row, pltpu.roll(h, TM - 1, 0))
    xx = 0.5 * (h_prev + h_next) - h
    mu = mu_ref[...]

    def mix(n):
        return (h + xx * mu[n:n + 1]).astype(BF16)

    r = _dot(mix(0), wrkv_ref[0])
    k = _dot(mix(1), wrkv_ref[1])
    v = _dot(mix(2), wrkv_ref[2])
    tw = jnp.tanh(_dot(mix(3), w1_ref[...])).astype(BF16)
    ua = _dot(mix(4), a1_ref[...]).astype(BF16)
    g = _dot(_sigmoid(_dot(mix(5), g1_ref[...])).astype(BF16), g2_ref[...])
    kk = k * kk_ref_[...]
    n2 = _seg_sum(kk * kk, seg_ref[...], segt_ref[...])
    kk = kk / jnp.maximum(jnp.sqrt(n2), 1e-12)
    r_o[...] = r
    v_o[...] = v
    kk_o[...] = kk
    g_o[...] = g
    ka = ka_ref[...]
    for z in range(2):
        w_pre = w0_ref[z:z + 1, :] + _dot(tw, w2_ref[z])
        lw_o[z] = NEG_EXP_M05 * _sigmoid(w_pre)
        a =_sigmoid(a0_ref[z:z + 1, :] + _dot(ua, a2_ref[z]))
        kd_o[z] = k * (1.0 + (a - 1.0) * ka)
        b_o[z] = kk * a


def _rwkv_inputs(x2, mod, gain, p, seg, segt, tiles_per_sample):
    n, d = x2.shape
    nt = n // TM
    blk8 = TM // 8
    last8 = n // 8 - 1
    row_spec = pl.BlockSpec((TM, d), lambda i: (i, 0))
    dir_spec = pl.BlockSpec((2, TM, d), lambda i: (0, i, 0))
    tok = jax.ShapeDtypeStruct((n, d), F32)
    tok2 = jax.ShapeDtypeStruct((2, n, d), F32)
    return pl.pallas_call(
        functools.partial(_rwkv_in_kernel, tiles_per_sample=tiles_per_sample),
        out_shape=(tok, tok, tok, tok, tok2, tok2, tok2),
        grid=(nt,),
        in_specs=[
            row_spec,
            pl.BlockSpec((8, d), lambda i: (jnp.maximum(i * blk8 - 1, 0), 0)),
            pl.BlockSpec((8, d), lambda i: (jnp.minimum((i + 1) * blk8, last8), 0)),
            pl.BlockSpec((1, N_MOD, d), lambda i: (_mod_row(i, tiles_per_sample), 0, 0)),
            _const_spec((1, d)),
            _const_spec((6, d)),
            _const_spec((3, d, d)),
            _const_spec((d, LANES)),
            _const_spec((2, LANES, d)),
            _const_spec((2, d)),
            _const_spec((d, LANES)),
            _const_spec((2, LANES, d)),
            _const_spec((2, d)),
            _const_spec((d, LANES)),
            _const_spec((LANES, d)),
            _const_spec((1, d)),
            _const_spec((1, d)),
            _const_spec((d, LANES)),
            _const_spec((LANES, d)),
        ],
        out_specs=(row_spec, row_spec, row_spec, row_spec, dir_spec, dir_spec, dir_spec),
        compiler_params=_cparams("arbitrary"),
        name="rwkv_inputs",
    )(x2, x2, x2, mod, gain, p["mu"], p["w_rkv"], p["w1"], p["w2"], p["w0"], p["a1"], p["a2"], p["a0"],
      p["g1"], p["g2"], p["k_k"], p["k_a"], seg, segt)


def _scan_consts(reverse):
    sgn = -1 if reverse else 1
    n2 = 2 * CHUNK
    row = lax.broadcasted_iota(jnp.int32, (n2, n2), 0)
    col = lax.broadcasted_iota(jnp.int32, (n2, n2), 1)
    same = (row // CHUNK) == (col // CHUNK)
    dt = (row % CHUNK - col % CHUNK) * sgn
    rc = lax.broadcasted_iota(jnp.int32, (CHUNK, CHUNK), 0)
    cc = lax.broadcasted_iota(jnp.int32, (CHUNK, CHUNK), 1)
    return {
        "same": same,
        "strict": jnp.logical_and(same, dt > 0),
        "incl": jnp.logical_and(same, dt >= 0),
        "eye": (row == col).astype(F32),
        "tri": ((rc - cc) * sgn >= 0).astype(BF16),
        "head0": lax.broadcasted_iota(jnp.int32, (CHUNK, LANES), 1) < HEAD_DIM,
    }


def _scan_units(units):
    n2 = 2 * CHUNK
    every = lambda f: [f(u) for u in units]

    def stack(u, x):
        return jnp.concatenate([jnp.where(u["c"]["head0"], x, 0.0), jnp.where(u["c"]["head0"], 0.0, x)], axis=0)

    def prep(u):
        cl = _dot_exact_rhs_left(u["c"]["tri"], u["lw"])
        tot = jnp.sum(u["lw"], axis=0, keepdims=True)
        e_ncl = jnp.exp(-cl)
        e_end = jnp.exp(tot - cl)
        u["q2"] = jnp.concatenate([stack(u, u["kk"] * jnp.exp(cl - u["lw"])), stack(u, u["r"] * jnp.exp(cl))], axis=0)
        u["k2"] = jnp.concatenate([stack(u, u["kd"] * e_ncl), stack(u, u["b"] * e_ncl)], axis=0)
        u["ket"] = jnp.concatenate([u["kd"] * e_end, -(u["b"] * e_end)], axis=0).T
        u["g_col"] = jnp.sum(jnp.where(u["c"]["eye"] > 0, jnp.exp(tot), 0.0), axis=1, keepdims=True)
        u["vs"] = stack(u, u["v"])

    every(prep)
    a_all = every(lambda u: _mm(u["q2"], u["k2"], nt=True))
    qm = every(lambda u: _mm(u["q2"], u["m0"]))
    for u, a in zip(units, a_all):
        c = u["c"]
        u["l_kk"] = jnp.where(c["strict"], a[:n2, :n2], 0.0)
        u["l_rk"] = jnp.where(c["incl"], a[n2:, :n2], 0.0)
        u["l_rb"] = jnp.where(c["incl"], a[n2:, n2:], 0.0)
        u["pw"] = -jnp.where(c["strict"], a[:n2, n2:], 0.0)
        u["inv"] = c["eye"] + u["pw"]
    lv = every(lambda u: _mm(u["l_kk"], u["vs"]))
    levels = CHUNK.bit_length() - 1
    sq = every(lambda u: _mm(u["pw"], u["pw"]))
    for u, x in zip(units, sq):
        u["pw"] = x
    for _ in range(1, levels - 1):
        st = every(lambda u: _mm(jnp.concatenate([u["pw"], u["inv"]], axis=0), u["pw"]))
        for u, x in zip(units, st):
            u["pw"] = x[:n2]
            u["inv"] = u["inv"] + x[n2:]
    last = every(lambda u: _mm(u["inv"], u["pw"]))
    for u, x in zip(units, last):
        u["inv"] = u["inv"] + x
    us = [_mm(u["inv"], q[:n2] + t) for u, q, t in zip(units, qm, lv)]
    ys = [q[n2:] + _mm(jnp.concatenate([u["l_rk"], -u["l_rb"]], axis=1), jnp.concatenate([u["vs"], s], axis=0))
          for u, q, s in zip(units, qm, us)]
    new = [_mm(u["ket"], jnp.concatenate([u["v"], s[:CHUNK] + s[CHUNK:]], axis=0)) for u, s in zip(units, us)]
    ms = [u["m0"] * u["g_col"] + jnp.where(u["c"]["same"], x, 0.0) for u, x in zip(units, new)]
    return [y[:CHUNK] + y[CHUNK:] for y in ys], ms


def _dot_exact_rhs_left(a_bf16, b):
    h, m, l = _split3(b)
    return _dot(a_bf16, h) + (_dot(a_bf16, m) + _dot(a_bf16, l))


def _scan_kernel(rf_ref, vf_ref, kkf_ref, rr_ref, vr_ref, kkr_ref, lwf_ref, kdf_ref, bf_ref, lwr_ref, kdr_ref,
                 br_ref, yf_ref, yr_ref, m_ref, *, pairs):
    @pl.when(pl.program_id(2) == 0)
    def _():
        m_ref[...] = jnp.zeros_like(m_ref)

    dirs = (
        (_scan_consts(False), rf_ref, vf_ref, kkf_ref, lwf_ref, kdf_ref, bf_ref, yf_ref),
        (_scan_consts(True), rr_ref, vr_ref, kkr_ref, lwr_ref, kdr_ref, br_ref, yr_ref),
    )
    units = []
    for z, (consts, r_ref, v_ref, kk_ref, lw_ref, kd_ref, b_ref, _) in enumerate(dirs):
        for p in range(pairs):
            sl = slice(p * LANES, (p + 1) * LANES)
            units.append({"c": consts, "r": r_ref[:, sl], "v": v_ref[:, sl], "kk": kk_ref[:, sl], "lw": lw_ref[0, :, sl],
                          "kd": kd_ref[0, :, sl], "b": b_ref[0, :, sl], "m0": m_ref[z * pairs + p]})
    ys, ms = _scan_units(units)
    for i, (y, m1) in enumerate(zip(ys, ms)):
        z, p = divmod(i, pairs)
        dirs[z][-1][0, :, p * LANES:(p + 1) * LANES] = y
        m_ref[i] = m1


def _rwkv_scan(r, v, kk, lw, kd, b, batch, ctx_len, pairs=8):
    n, d = r.shape
    tt = n // batch
    nch = tt // CHUNK
    nch_ctx = ctx_len // CHUNK
    width = pairs * LANES
    groups = d // width

    fwd_row = lambda bb, c: bb * nch + c
    rev_row = lambda bb, c: bb * nch + jnp.where(c < nch_ctx, nch_ctx - 1 - c, nch + nch_ctx - 1 - c)
    shared = lambda row: pl.BlockSpec((CHUNK, width), lambda bb, g, c: (row(bb, c), g))
    per_dir = lambda z, row: pl.BlockSpec((1, CHUNK, width), lambda bb, g, c: (z, row(bb, c), g))
    return pl.pallas_call(
        functools.partial(_scan_kernel, pairs=pairs),
        out_shape=(jax.ShapeDtypeStruct((1, n, d), F32), jax.ShapeDtypeStruct((1, n, d), F32)),
        grid=(batch, groups, nch),
        in_specs=[shared(fwd_row)] * 3 + [shared(rev_row)] * 3 + [per_dir(0, fwd_row)] * 3 + [per_dir(1, rev_row)] * 3,
        out_specs=(per_dir(0, fwd_row), per_dir(0, rev_row)),
        scratch_shapes=[pltpu.VMEM((2 * pairs, LANES, LANES), F32)],
        compiler_params=_cparams("arbitrary", "arbitrary", "arbitrary"),
        name="rwkv_scan",
    )(r, v, kk, r, v, kk, lw, kd, b, lw, kd, b)


def _rwkv_out_kernel(y0_ref, y1_ref, r_ref, kd0_ref, kd1_ref, v_ref, g_ref, x_ref, m_ref, rk_ref, lnw_ref,
                     lnb_ref, wo_ref, seg_ref, segt_ref, o_ref):
    seg, segt = seg_ref[...], segt_ref[...]
    y = y0_ref[0] + y1_ref[0]
    mean = _seg_sum(y, seg, segt) * (1.0 / HEAD_DIM)
    dy = y - mean
    var = _seg_sum(dy * dy, seg, segt) * (1.0 / HEAD_DIM)
    yn = (dy * lax.rsqrt(var + GN_EPS)) * lnw_ref[...] + lnb_ref[...]
    bonus = _seg_sum(r_ref[...] * (kd0_ref[0] + kd1_ref[0]) * rk_ref[...], seg, segt)
    out = (yn + bonus * v_ref[...]) * g_ref[...]
    o = _dot(out.astype(BF16), wo_ref[...])
    o_ref[...] = x_ref[...] + m_ref[0][2:3] * o


def _rwkv_readout(yf, yr, r, kd, v, g, x2, mod, p, seg, segt, tiles_per_sample):
    n, d = x2.shape
    row_spec = pl.BlockSpec((TM, d), lambda i: (i, 0))
    return pl.pallas_call(
        _rwkv_out_kernel,
        out_shape=jax.ShapeDtypeStruct((n, d), F32),
        grid=(n // TM,),
        in_specs=[
            pl.BlockSpec((1, TM, d), lambda i: (0, i, 0)),
            pl.BlockSpec((1, TM, d), lambda i: (0, i, 0)),
            row_spec,
            pl.BlockSpec((1, TM, d), lambda i: (0, i, 0)),
            pl.BlockSpec((1, TM, d), lambda i: (1, i, 0)),
            row_spec, row_spec, row_spec,
            pl.BlockSpec((1, N_MOD, d), lambda i: (_mod_row(i, tiles_per_sample), 0, 0)),
            _const_spec((1, d)), _const_spec((1, d)), _const_spec((1, d)),
            _const_spec((d, d)),
            _const_spec((d, LANES)), _const_spec((LANES, d)),
        ],
        out_specs=row_spec,
        compiler_params=_cparams("arbitrary"),
        name="rwkv_readout",
    )(yf, yr, r, kd, kd, v, g, x2, mod, p["r_k"], p["ln_w"], p["ln_b"], p["w_o"], seg, segt)


def _rope(x, cos, sin_lo, sin_hi):
    w = x.shape[1]
    reps = w // LANES
    tile = lambda t: jnp.concatenate([t] * reps, axis=1) if reps > 1 else t
    half = HEAD_DIM // 4
    return x * tile(cos) + pltpu.roll(x, w - half, 1) * tile(sin_lo) + pltpu.roll(x, half, 1) * tile(sin_hi)


def _gqa_proj_kernel(x_ref, m_ref, gain_ref, w_ref, qg_ref, kg_ref, cos_ref, slo_ref, shi_ref, seg_ref, segt_ref,
                     segk_ref, segtk_ref, q_o, k_o, v_o, *, d, dkv):
    m = m_ref[0]
    h = _norm_mod(x_ref[...], gain_ref[...], m[0:1], m[1:2]).astype(BF16)
    qkv = _dot(h, w_ref[...])
    q, k, v = qkv[:, :d], qkv[:, d:d + dkv], qkv[:, d + dkv:]
    cos, slo, shi = cos_ref[...], slo_ref[...], shi_ref[...]
    qms = _seg_sum(q * q, seg_ref[...], segt_ref[...]) * (1.0 / HEAD_DIM)
    q = (q * lax.rsqrt(qms + NORM_EPS)) * qg_ref[...]
    kms = _seg_sum(k * k, segk_ref[...], segtk_ref[...]) * (1.0 / HEAD_DIM)
    k = (k * lax.rsqrt(kms + NORM_EPS)) * kg_ref[...]
    q = (_rope(q, cos, slo, shi) * (HEAD_DIM ** -0.5 * LOG2_E)).astype(BF16)
    k = _rope(k, cos, slo, shi).astype(BF16)
    v = v.astype(BF16)
    lane = lax.broadcasted_iota(jnp.int32, (v.shape[0], LANES - HEAD_DIM), 1)
    ones_pad = jnp.where(lane == 0, 1.0, 0.0).astype(BF16)
    for hh in range(d // HEAD_DIM):
        q_o[hh] = q[:, hh * HEAD_DIM:(hh + 1) * HEAD_DIM]
    for hh in range(dkv // HEAD_DIM):
        k_o[hh] = k[:, hh * HEAD_DIM:(hh + 1) * HEAD_DIM]
        v_o[hh] = jnp.concatenate([v[:, hh * HEAD_DIM:(hh + 1) * HEAD_DIM], ones_pad], axis=1)


def _gqa_project(x2, mod, gain, p, rope, seg, segt, segk, segtk, tiles_per_sample):
    n, d = x2.shape
    dkv = ATTN_KV_HEADS * HEAD_DIM
    nq, nkv = d // HEAD_DIM, ATTN_KV_HEADS
    heads = lambda h: pl.BlockSpec((h, TM, HEAD_DIM), lambda i: (0, i, 0))
    pos = pl.BlockSpec((TM, LANES), lambda i: (i % tiles_per_sample, 0))
    return pl.pallas_call(
        functools.partial(_gqa_proj_kernel, d=d, dkv=dkv),
        out_shape=(jax.ShapeDtypeStruct((nq, n, HEAD_DIM), BF16), jax.ShapeDtypeStruct((nkv, n, HEAD_DIM), BF16),
                   jax.ShapeDtypeStruct((nkv, n, LANES), BF16)),
        grid=(n // TM,),
        in_specs=[
            pl.BlockSpec((TM, d), lambda i: (i, 0)),
            pl.BlockSpec((1, N_MOD, d), lambda i: (_mod_row(i, tiles_per_sample), 0, 0)),
            _const_spec((1, d)),
            _const_spec((d, d + 2 * dkv)),
            _const_spec((1, d)), _const_spec((1, dkv)),
            pos, pos, pos,
            _const_spec((d, LANES)), _const_spec((LANES, d)),
            _const_spec((dkv, LANES)), _const_spec((LANES, dkv)),
        ],
        out_specs=(heads(nq), heads(nkv), pl.BlockSpec((nkv, TM, LANES), lambda i: (0, i, 0))),
        compiler_params=_cparams("arbitrary"),
        name="gqa_project",
    )(x2, mod, gain, p["w_qkv"], p["q_gain"], p["k_gain"], rope[0], rope[1], rope[2], seg, segt, segk, segtk)


def _attn_kernel(q_ref, k_ref, v_ref, o_ref, *, group, kvs, ctx_rows):
    def attend(kv_rows):
        for kv in range(kvs):
            k = k_ref[kv, :kv_rows, :]
            v = v_ref[kv, :kv_rows, :]
            for hh in range(kv * group, (kv + 1) * group):
                s = _dot_nt(q_ref[hh], k)
                pr = jnp.exp2(s - jnp.max(s, axis=-1, keepdims=True))
                oe = _dot(pr.astype(BF16), v)
                o_ref[hh] = (oe[:, :HEAD_DIM] / oe[:, HEAD_DIM:HEAD_DIM + 1]).astype(BF16)

    is_ctx = pl.program_id(2) == 0
    pl.when(is_ctx)(lambda: attend(ctx_rows))
    pl.when(jnp.logical_not(is_ctx))(lambda: attend(k_ref.shape[1]))


def _attention(q, k, v, batch, tiles_per_sample, ctx_rows, kvs=2):
    nq, n, _ = q.shape
    nkv = k.shape[0]
    group = nq // nkv
    tt = n // batch
    q_spec = pl.BlockSpec((kvs * group, TM, HEAD_DIM), lambda bb, g, t: (g, bb * tiles_per_sample + t, 0))
    return pl.pallas_call(
        functools.partial(_attn_kernel, group=group, kvs=kvs, ctx_rows=ctx_rows),
        out_shape=jax.ShapeDtypeStruct((nq, n, HEAD_DIM), BF16),
        grid=(batch, nkv // kvs, tiles_per_sample),
        in_specs=[q_spec, pl.BlockSpec((kvs, tt, HEAD_DIM), lambda bb, g, t: (g, bb, 0)),
                  pl.BlockSpec((kvs, tt, LANES), lambda bb, g, t: (g, bb, 0))],
        out_specs=q_spec,
        compiler_params=_cparams("arbitrary", "arbitrary", "arbitrary"),
        name="gqa_attention",
    )(q, k, v)


def _attn_out_kernel(o_ref, x_ref, m_ref, wo_ref, y_ref):
    o = jnp.concatenate([o_ref[hh] for hh in range(o_ref.shape[0])], axis=1)
    y_ref[...] = x_ref[...] + m_ref[0][2:3] * _dot(o, wo_ref[...])


def _attn_out(o, x2, mod, w_o, tiles_per_sample):
    n, d = x2.shape
    nq = o.shape[0]
    row_spec = pl.BlockSpec((TM, d), lambda i: (i, 0))
    return pl.pallas_call(
        _attn_out_kernel,
        out_shape=jax.ShapeDtypeStruct((n, d), F32),
        grid=(n // TM,),
        in_specs=[
            pl.BlockSpec((nq, TM, HEAD_DIM), lambda i: (0, i, 0)),
            row_spec,
            pl.BlockSpec((1, N_MOD, d), lambda i: (_mod_row(i, tiles_per_sample), 0, 0)),
            _const_spec((d, d)),
        ],
        out_specs=row_spec,
        compiler_params=_cparams("arbitrary"),
        name="gqa_out_proj",
    )(o, x2, mod, w_o)


def _router_kernel(x_ref, m_ref, gain_ref, wr_ref, f_o, afft_o):
    m = m_ref[0]
    f = _norm_mod(x_ref[...], gain_ref[...], m[3:4], m[4:5])
    f_o[...] = f
    logits = _mm3(f, wr_ref[...])
    lane = lax.broadcasted_iota(jnp.int32, logits.shape, 1)
    logits = jnp.where(lane < N_EXPERTS, logits, -1e30)
    e = jnp.exp(logits - jnp.max(logits, axis=-1, keepdims=True))
    aff = e / jnp.sum(e, axis=-1, keepdims=True)
    afft_o[0] = aff.T[:N_EXPERTS]


def _moe_router(x2, mod, gain, w_router, batch, tiles_per_sample):
    n, d = x2.shape
    tt = n // batch
    return pl.pallas_call(
        _router_kernel,
        out_shape=(jax.ShapeDtypeStruct((n, d), F32), jax.ShapeDtypeStruct((batch, N_EXPERTS, tt), F32)),
        grid=(n // TM,),
        in_specs=[
            pl.BlockSpec((TM, d), lambda i: (i, 0)),
            pl.BlockSpec((1, N_MOD, d), lambda i: (_mod_row(i, tiles_per_sample), 0, 0)),
            _const_spec((1, d)),
            _const_spec((d, LANES)),
        ],
        out_specs=(pl.BlockSpec((TM, d), lambda i: (i, 0)),
                   pl.BlockSpec((1, N_EXPERTS, TM), lambda i: (i // tiles_per_sample, 0, i % tiles_per_sample))),
        compiler_params=_cparams("arbitrary"),
        name="moe_router",
    )(x2, mod, gain, w_router)


def _prefix_excl(mask, tri_excl):
    xb = mask.astype(BF16)
    carry = jnp.zeros((mask.shape[0], 1), F32)
    outs = []
    for blk in range(mask.shape[1] // LANES):
        piece = xb[:, blk * LANES:(blk + 1) * LANES]
        outs.append(_dot(piece, tri_excl) + carry)
        carry = carry + jnp.sum(piece.astype(F32), axis=1, keepdims=True)
    return jnp.concatenate(outs, axis=1) if len(outs) > 1 else outs[0]


def _top_cap(a, cap, tri_excl):
    ai = lax.bitcast_convert_type(a, jnp.int32)

    def body(i, thr):
        cand = thr | jnp.left_shift(jnp.int32(1), F32_VALUE_BITS - 1 - i)
        cnt = jnp.sum((ai >= cand).astype(F32), axis=1, keepdims=True)
        return jnp.where(cnt >= cap, cand, thr)

    thr = lax.fori_loop(0, F32_VALUE_BITS, body, jnp.zeros((a.shape[0], 1), jnp.int32))
    gt = ai > thr
    eq = ai == thr
    need = cap - jnp.sum(gt.astype(F32), axis=1, keepdims=True)
    sel = jnp.logical_or(gt, jnp.logical_and(eq, _prefix_excl(eq, tri_excl) < need))
    return sel, _prefix_excl(sel, tri_excl)


def _select_kernel(afft_ref, tri_ref, idx_o, gate_o, sel_s, pos_s, *, sets):
    e = pl.program_id(1)

    @pl.when(e == 0)
    def _():
        for off, n, cap, _ in sets:
            sel, pos = _top_cap(afft_ref[0, :, off:off + n], cap, tri_ref[...])
            sel_s[:, off:off + n] = sel.astype(F32)
            pos_s[:, off:off + n] = pos

    for off, n, cap, slot0 in sets:
        a_e = afft_ref[0, pl.ds(e, 1), off:off + n]
        sel_e = sel_s[pl.ds(e, 1), off:off + n]
        pos_e = pos_s[pl.ds(e, 1), off:off + n]
        wl = min(cap, LANES)
        nw = cap // wl
        pos_i = pos_e.astype(jnp.int32)
        lane_of = jnp.bitwise_and(pos_i, wl - 1)
        win_of = jnp.right_shift(pos_i, wl.bit_length() - 1)
        lane_id = lax.broadcasted_iota(jnp.int32, (wl, n), 0)
        onehot = jnp.where(jnp.logical_and(lane_of == lane_id, sel_e > 0), 1.0, 0.0).astype(BF16)
        tok = lax.broadcasted_iota(jnp.int32, (BF16_ROWS, n), 1)
        rid = lax.broadcasted_iota(jnp.int32, (BF16_ROWS, n), 0)
        a_h, a_m, a_l = _split3(a_e)
        rows = jnp.where(rid == 0, jnp.right_shift(tok, TOK_RADIX.bit_length() - 1).astype(F32),
               jnp.where(rid == 1, jnp.bitwise_and(tok, TOK_RADIX - 1).astype(F32),
               jnp.where(rid == 2, a_h.astype(F32),
               jnp.where(rid == 3, a_m.astype(F32),
               jnp.where(rid == 4, a_l.astype(F32), 0.0)))))
        stacked = jnp.concatenate([jnp.where(win_of == wi, rows, 0.0) for wi in range(nw)], axis=0).astype(BF16)
        res_all = _dot_nt(stacked, onehot)
        for wi in range(nw):
            res = res_all[BF16_ROWS * wi:BF16_ROWS * (wi + 1)]
            lo = slot0 + wi * wl
            idx_o[0, 0, :, lo:lo + wl] = (res[0:1] * TOK_RADIX + res[1:2]).astype(jnp.int32) + off
            gate_o[0, 0, :, lo:lo + wl] = res[2:3] + (res[3:4] + res[4:5])


def _moe_select(afft, sets, n_slots):
    batch, ne, tt = afft.shape
    tri = (jnp.arange(LANES)[:, None] < jnp.arange(LANES)[None, :]).astype(BF16)
    slot_spec = pl.BlockSpec((1, 1, 1, n_slots), lambda bb, e: (bb, e, 0, 0))
    return pl.pallas_call(
        functools.partial(_select_kernel, sets=sets),
        out_shape=(jax.ShapeDtypeStruct((batch, ne, 1, n_slots), jnp.int32),
                   jax.ShapeDtypeStruct((batch, ne, 1, n_slots), F32)),
        grid=(batch, ne),
        in_specs=[pl.BlockSpec((1, ne, tt), lambda bb, e: (bb, 0, 0)), _const_spec((LANES, LANES))],
        out_specs=(slot_spec, slot_spec),
        scratch_shapes=[pltpu.VMEM((ne, tt), F32), pltpu.VMEM((ne, tt), F32)],
        compiler_params=_cparams("arbitrary", "arbitrary"),
        name="moe_select",
    )(afft, tri)


def _gather_kernel(idx_ref, f_ref, xe_o, buf, *, n_slots):
    def body(c, _):
        r = idx_ref[0, 0, 0, c]
        buf[pl.ds(c, 1), :] = f_ref[pl.ds(r, 1), :]
        return 0

    lax.fori_loop(0, n_slots, body, 0, unroll=8)
    xe_o[0, 0] = buf[...].astype(BF16)


def _moe_gather(idx, f, batch):
    n, d = f.shape
    tt = n // batch
    ne, n_slots = idx.shape[1], idx.shape[3]
    return pl.pallas_call(
        functools.partial(_gather_kernel, n_slots=n_slots),
        out_shape=jax.ShapeDtypeStruct((batch, ne, n_slots, d), BF16),
        grid=(batch, ne),
        in_specs=[
            pl.BlockSpec((1, 1, 1, n_slots), lambda bb, e: (bb, e, 0, 0), memory_space=pltpu.SMEM),
            pl.BlockSpec((tt, d), lambda bb, e: (bb, 0)),
        ],
        out_specs=pl.BlockSpec((1, 1, n_slots, d), lambda bb, e: (bb, e, 0, 0)),
        scratch_shapes=[pltpu.VMEM((n_slots, d), F32)],
        compiler_params=_cparams("arbitrary", "arbitrary"),
        name="moe_gather",
    )(idx, f)


def _expert_kernel(xe_ref, wg_ref, wu_ref, wd_ref, ml_ref, mc_ref, ye_o, *, cap_l):
    first = pl.program_id(2) == 0
    last = pl.program_id(2) == pl.num_programs(2) - 1
    wg = wg_ref[0, 0].astype(BF16)
    wu = wu_ref[0, 0].astype(BF16)
    wd = wd_ref[0, 0].astype(BF16)
    for i in range(xe_ref.shape[0]):
        x = xe_ref[i, 0]
        h = (_silu(_dot(x, wg)) * _dot(x, wu)).astype(BF16)
        y = _dot(h, wd)
        ye_o[i, 0] = jnp.where(first, y, ye_o[i, 0] + y)

    @pl.when(last)
    def _():
        latent_slot = lax.broadcasted_iota(jnp.int32, ye_o.shape[2:], 0) < cap_l
        for i in range(xe_ref.shape[0]):
            ye_o[i, 0] = ye_o[i, 0] * jnp.where(latent_slot, ml_ref[i], mc_ref[...])


def _moe_experts(xe, w_gate_up, w_down, gate_lat, gate_ctx, layer, cap_l, halves=2, tf=256):
    batch, ne, s, d = xe.shape
    de = w_down.shape[2]
    nf = de // tf
    bh = batch // halves
    tok = pl.BlockSpec((bh, 1, s, d), lambda e, mh, f: (mh, e, 0, 0))
    return pl.pallas_call(
        functools.partial(_expert_kernel, cap_l=cap_l),
        out_shape=jax.ShapeDtypeStruct((batch, ne, s, d), F32),
        grid=(ne, halves, nf),
        in_specs=[
            tok,
            pl.BlockSpec((1, 1, d, tf), lambda e, mh, f: (layer, e, 0, f)),
            pl.BlockSpec((1, 1, d, tf), lambda e, mh, f: (layer, e, 0, nf + f)),
            pl.BlockSpec((1, 1, tf, d), lambda e, mh, f: (layer, e, f, 0)),
            pl.BlockSpec((bh, 1, d), lambda e, mh, f: (mh, 0, 0)),
            _const_spec((1, d)),
        ],
        out_specs=tok,
        compiler_params=_cparams("arbitrary", "arbitrary", "arbitrary"),
        name="moe_experts",
    )(xe, w_gate_up, w_gate_up, w_down, gate_lat, gate_ctx)


def _combine_kernel(idx_ref, gate_ref, ye_ref, x_hbm, out_o, sem, *, n_slots):
    bb = pl.program_id(0)
    rows = out_o.shape[0]

    @pl.when(pl.program_id(1) == 0)
    def _():
        seed = pltpu.make_async_copy(x_hbm.at[pl.ds(bb * rows, rows)], out_o, sem)
        seed.start()
        seed.wait()

    def body(c, _):
        r = idx_ref[0, 0, 0, c]
        out_o[pl.ds(r, 1), :] += gate_ref[0, 0, 0, c] * ye_ref[0, 0, pl.ds(c, 1), :]
        return 0

    lax.fori_loop(0, n_slots, body, 0, unroll=8)


def _moe_combine(idx, gate, ye, x2, tt):
    batch, ne, n_slots, d = ye.shape
    smem = pl.BlockSpec((1, 1, 1, n_slots), lambda bb, e: (bb, e, 0, 0), memory_space=pltpu.SMEM)
    return pl.pallas_call(
        functools.partial(_combine_kernel, n_slots=n_slots),
        out_shape=jax.ShapeDtypeStruct((batch * tt, d), F32),
        grid=(batch, ne),
        in_specs=[smem, smem, pl.BlockSpec((1, 1, n_slots, d), lambda bb, e: (bb, e, 0, 0)),
                  pl.BlockSpec(memory_space=pl.ANY)],
        out_specs=pl.BlockSpec((tt, d), lambda bb, e: (bb, 0)),
        scratch_shapes=[pltpu.SemaphoreType.DMA(())],
        compiler_params=_cparams("arbitrary", "arbitrary"),
        name="moe_combine",
    )(idx, gate, ye, x2)


def _final_kernel(x_ref, g_ref, o_ref):
    x = x_ref[...]
    ms = jnp.mean(x * x, axis=-1, keepdims=True)
    o_ref[0] = (x * lax.rsqrt(ms + NORM_EPS)) * g_ref[...]


def _final_norm(x2, gain, batch, tiles_per_sample):
    n, d = x2.shape
    lat_tiles = tiles_per_sample - 1
    return pl.pallas_call(
        _final_kernel,
        out_shape=jax.ShapeDtypeStruct((batch, lat_tiles * TM, d), F32),
        grid=(batch, lat_tiles),
        in_specs=[pl.BlockSpec((TM, d), lambda bb, t: (bb * tiles_per_sample + 1 + t, 0)), _const_spec((1, d))],
        out_specs=pl.BlockSpec((1, TM, d), lambda bb, t: (bb, t, 0)),
        compiler_params=_cparams("arbitrary", "arbitrary"),
        name="final_norm",
    )(x2, gain)


def _segment_matrices(width):
    heads = jnp.arange(width) // HEAD_DIM
    seg = (heads[:, None] == jnp.arange(LANES)[None, :]).astype(BF16)
    return seg, seg.T


def _rope_tables(seq, ctx_len):
    t = jnp.arange(seq)
    pos = jnp.stack([(t // GRID_W).astype(F32), (t % GRID_W).astype(F32)], axis=-1)
    half = HEAD_DIM // 4
    inv_freq = ROPE_THETA ** (-jnp.arange(0, 2 * half, 2, dtype=F32) / (2 * half))
    ang = pos[:, :, None] * inv_freq
    cos, sin = jnp.cos(ang), jnp.sin(ang)
    zero = jnp.zeros_like(sin)
    per_head = lambda first, second: jnp.concatenate([first, second], axis=-1).reshape(seq, HEAD_DIM)
    cos_h = per_head(cos, cos)
    sin_lo = per_head(-sin, zero)
    sin_hi = per_head(zero, sin)
    pad = lambda tab, fill: jnp.concatenate([jnp.full((ctx_len, HEAD_DIM), fill, F32), tab], axis=0)
    two = lambda tab: jnp.concatenate([tab, tab], axis=1)
    return two(pad(cos_h, 1.0)), two(pad(sin_lo, 0.0)), two(pad(sin_hi, 0.0))


def kernel(x, c, ctx, c_ctx, mod_w, mod_b, norm_mix, norm_ffn, rwkv_mu, rwkv_w_rkv, rwkv_w0, rwkv_w1, rwkv_w2,
           rwkv_a0, rwkv_a1, rwkv_a2, rwkv_g1, rwkv_g2, rwkv_k_k, rwkv_k_a, rwkv_r_k, rwkv_ln_w, rwkv_ln_b,
           rwkv_w_o, attn_w_qkv, attn_q_gain, attn_k_gain, attn_w_o, moe_router, moe_w_gate_up, moe_w_down,
           final_norm):
    batch, seq, d = x.shape
    ctx_len = ctx.shape[1]
    depth = mod_w.shape[0]
    tt = ctx_len + seq
    tps = tt // TM
    assert ctx_len == TM and seq % TM == 0 and CHUNK == HEAD_DIM and batch + 1 <= 16
    nq = d // HEAD_DIM

    x2 = jnp.concatenate([ctx, x], axis=1).reshape(batch * tt, d)
    cc = jnp.zeros((16, d), F32).at[0].set(c_ctx).at[1:batch + 1].set(c)
    mods = _mod_vectors(cc, mod_w, mod_b).reshape(depth, 16, N_MOD, d)
    seg, segt = _segment_matrices(d)
    segk, segtk = _segment_matrices(ATTN_KV_HEADS * HEAD_DIM)
    rope = _rope_tables(seq, ctx_len)
    cap_l = EC_CAPACITY * seq // N_EXPERTS
    cap_c = EC_CAPACITY * ctx_len // N_EXPERTS
    sets = ((ctx_len, seq, cap_l, 0), (0, ctx_len, cap_c, cap_l))
    router_pad = jnp.zeros((depth, d, LANES), F32).at[:, :, :N_EXPERTS].set(moe_router)

    ia = ib = 0
    for i in range(depth):
        mod = mods[i]
        if i % 2 == 0:
            zpad = jnp.zeros((HEAD_DIM, d), F32)
            lora_pad = lambda w: jnp.stack([jnp.concatenate([w[0], zpad], 0), jnp.concatenate([zpad, w[1]], 0)]).astype(BF16)
            p = {
                "mu": rwkv_mu[ia], "w_rkv": rwkv_w_rkv[ia].astype(BF16),
                "w1": jnp.concatenate([rwkv_w1[ia, 0], rwkv_w1[ia, 1]], axis=1).astype(BF16), "w2": lora_pad(rwkv_w2[ia]),
                "w0": rwkv_w0[ia],
                "a1": jnp.concatenate([rwkv_a1[ia, 0], rwkv_a1[ia, 1]], axis=1).astype(BF16), "a2": lora_pad(rwkv_a2[ia]),
                "a0": rwkv_a0[ia],
                "g1": rwkv_g1[ia].astype(BF16), "g2": rwkv_g2[ia].astype(BF16),
                "k_k": rwkv_k_k[ia].reshape(1, d), "k_a": rwkv_k_a[ia].reshape(1, d),
                "r_k": rwkv_r_k[ia].reshape(1, d), "ln_w": rwkv_ln_w[ia].reshape(1, d), "ln_b": rwkv_ln_b[ia].reshape(1, d),
                "w_o": rwkv_w_o[ia].astype(BF16),
            }
            r, v, kk, g, lw, kd, b = _rwkv_inputs(x2, mod, norm_mix[i].reshape(1, d), p, seg, segt, tps)
            yf, yr = _rwkv_scan(r, v, kk, lw, kd, b, batch, ctx_len)
            x2 = _rwkv_readout(yf, yr, r, kd, v, g, x2, mod, p, seg, segt, tps)
            ia += 1
        else:
            p = {
                "w_qkv": attn_w_qkv[ib].astype(BF16),
                "q_gain": jnp.tile(attn_q_gain[ib], nq).reshape(1, d),
                "k_gain": jnp.tile(attn_k_gain[ib], ATTN_KV_HEADS).reshape(1, ATTN_KV_HEADS * HEAD_DIM),
            }
            q, k, v = _gqa_project(x2, mod, norm_mix[i].reshape(1, d), p, rope, seg, segt, segk, segtk, tps)
            o = _attention(q, k, v, batch, tps, ctx_len)
            x2 = _attn_out(o, x2, mod, attn_w_o[ib].astype(BF16), tps)
            ib += 1
        f, afft = _moe_router(x2, mod, norm_ffn[i].reshape(1, d), router_pad[i], batch, tps)
        layer_sets = sets if i < depth - 1 else sets[:1]
        idx, gate = _moe_select(afft, layer_sets, sum(s[2] for s in layer_sets))
        xe = _moe_gather(idx, f, batch)
        ye = _moe_experts(xe, moe_w_gate_up, moe_w_down, mod[1:batch + 1, 5:6], mod[0, 5:6], i, cap_l)
        x2 = _moe_combine(idx, gate, ye, x2, tt)
    return _final_norm(x2, final_norm.reshape(1, d), batch, tps)
```

```python
import functools

import jax
import jax.numpy as jnp
from jax import lax
from jax.experimental import pallas as pl
from jax.experimental.pallas import tpu as pltpu

F32 = jnp.float32
BF16 = jnp.bfloat16

HEAD_DIM = 64
N_MOD = 6
NORM_EPS = 1e-6
GN_EPS = 64e-5
ROPE_THETA = 10000.0
GRID_W = 64
ATTN_KV_HEADS = 4
N_EXPERTS = 16
EC_CAPACITY = 2
TM = 256
CHUNK = 64
LANES = 128
VMEM_LIMIT = 56 * 1024 * 1024
NEG_EXP_M05 = -0.6065306597126334
LOG2_E = 1.4426950408889634
F32_VALUE_BITS = 31
BF16_ROWS = 16
TOK_RADIX = 64
SCATTER_GROUP = 8


def _cparams(*sem):
    return pltpu.CompilerParams(dimension_semantics=sem, vmem_limit_bytes=VMEM_LIMIT)


def _split2(x):
    hi = x.astype(BF16)
    lo = (x - hi.astype(F32)).astype(BF16)
    return hi, lo


def _split3(x):
    hi = x.astype(BF16)
    r1 = x - hi.astype(F32)
    mid = r1.astype(BF16)
    lo = (r1 - mid.astype(F32)).astype(BF16)
    return hi, mid, lo


def _dot(a, b):
    return jnp.dot(a, b, preferred_element_type=F32)


def _dot_nt(a, b):
    return lax.dot_general(a, b, (((1,), (1,)), ((), ())), preferred_element_type=F32)


def _mm(a, b, nt=False):
    return (_dot_nt if nt else _dot)(a.astype(BF16), b.astype(BF16))


def _mm3(a, b):
    ah, al = _split2(a)
    bh, bl = _split2(b)
    return _dot(ah, bh) + (_dot(ah, bl) + _dot(al, bh))


def _dot_exact_rhs(a, b_bf16):
    h, l = _split2(a)
    return _dot(h, b_bf16) + _dot(l, b_bf16)


def _seg_sum(x, seg, segt):
    s = _dot_exact_rhs(x, seg)
    return _dot_exact_rhs(s, segt)


def _norm_mod(x, gain, shift, scale):
    ms = jnp.mean(x * x, axis=-1, keepdims=True)
    y = x * lax.rsqrt(ms + NORM_EPS)
    return (y * gain) * (1.0 + scale) + shift


def _sigmoid(x):
    return 0.5 * jnp.tanh(0.5 * x) + 0.5


def _silu(x):
    return x * _sigmoid(x)


def _mod_row(i, tiles_per_sample):
    return jnp.where(i % tiles_per_sample == 0, 0, 1 + i // tiles_per_sample)


def _const_spec(shape):
    nd = len(shape)
    return pl.BlockSpec(shape, lambda *_: (0,) * nd)


def _mod_kernel(c_ref, w_ref, b_ref, o_ref):
    s = _silu(c_ref[...])
    o_ref[0] = _mm3(s, w_ref[0]) + b_ref[0]


def _mod_vectors(cc, mod_w, mod_b):
    depth, d, n = mod_w.shape
    tn = 1536
    return pl.pallas_call(
        _mod_kernel,
        out_shape=jax.ShapeDtypeStruct((depth, 16, n), F32),
        grid=(depth, n // tn),
        in_specs=[
            pl.BlockSpec((16, d), lambda l, j: (0, 0)),
            pl.BlockSpec((1, d, tn), lambda l, j: (l, 0, j)),
            pl.BlockSpec((1, 1, tn), lambda l, j: (l, 0, j)),
        ],
        out_specs=pl.BlockSpec((1, 16, tn), lambda l, j: (l, 0, j)),
        compiler_params=_cparams("arbitrary", "arbitrary"),
        name="mod_vectors",
    )(cc, mod_w, mod_b.reshape(depth, 1, n))


def _rwkv_in_kernel(x_ref, xp_ref, xn_ref, m_ref, gain_ref, mu_ref, wrkv_ref, w1_ref, w2_ref, w0_ref,
                    a1_ref, a2_ref, a0_ref, g1_ref, g2_ref, kk_ref_, ka_ref, seg_ref, segt_ref,
                    r_o, v_o, kk_o, g_o, lw_o, kd_o, b_o, *, tiles_per_sample):
    i = pl.program_id(0)
    j = i % tiles_per_sample
    m = m_ref[0]
    shift, scale = m[0:1], m[1:2]
    gain = gain_ref[...]
    h = _norm_mod(x_ref[...], gain, shift, scale)
    has_prev = (j >= 2).astype(F32)
    has_next = jnp.logical_and(j >= 1, j <= tiles_per_sample - 2).astype(F32)
    hp_row = _norm_mod(xp_ref[7:8, :], gain, shift, scale) * has_prev
    hn_row = _norm_mod(xn_ref[0:1, :], gain, shift, scale) * has_next
    row = lax.broadcasted_iota(jnp.int32, h.shape, 0)
    h_prev = jnp.where(row == 0, hp_row, pltpu.roll(h, 1, 0))
    h_next = jnp.where(row == TM - 1, hn_row, pltpu.roll(h, TM - 1, 0))
    xx = 0.5 * (h_prev + h_next) - h
    mu = mu_ref[...]

    def mix(n):
        return (h + xx * mu[n:n + 1]).astype(BF16)

    r = _dot(mix(0), wrkv_ref[0])
    k = _dot(mix(1), wrkv_ref[1])
    v = _dot(mix(2), wrkv_ref[2])
    tw = jnp.tanh(_dot(mix(3), w1_ref[...])).astype(BF16)
    ua = _dot(mix(4), a1_ref[...]).astype(BF16)
    g = _dot(_sigmoid(_dot(mix(5), g1_ref[...])).astype(BF16), g2_ref[...])
    kk = k * kk_ref_[...]
    n2 = _seg_sum(kk * kk, seg_ref[...], segt_ref[...])
    kk = kk / jnp.maximum(jnp.sqrt(n2), 1e-12)
    r_o[...] = r
    v_o[...] = v
    kk_o[...] = kk
    g_o[...] = g
    ka = ka_ref[...]
    for z in range(2):
        w_pre = w0_ref[z:z + 1, :] + _dot(tw, w2_ref[z])
        lw_o[z] = NEG_EXP_M05 * _sigmoid(w_pre)
        a =_sigmoid(a0_ref[z:z + 1, :] + _dot(ua, a2_ref[z]))
        kd_o[z] = k * (1.0 + (a - 1.0) * ka)
        b_o[z] = kk * a


def _rwkv_inputs(x2, mod, gain, p, seg, segt, tiles_per_sample):
    n, d = x2.shape
    nt = n // TM
    blk8 = TM // 8
    last8 = n // 8 - 1
    row_spec = pl.BlockSpec((TM, d), lambda i: (i, 0))
    dir_spec = pl.BlockSpec((2, TM, d), lambda i: (0, i, 0))
    tok = jax.ShapeDtypeStruct((n, d), F32)
    tok2 = jax.ShapeDtypeStruct((2, n, d), F32)
    return pl.pallas_call(
        functools.partial(_rwkv_in_kernel, tiles_per_sample=tiles_per_sample),
        out_shape=(tok, tok, tok, tok, tok2, tok2, tok2),
        grid=(nt,),
        in_specs=[
            row_spec,
            pl.BlockSpec((8, d), lambda i: (jnp.maximum(i * blk8 - 1, 0), 0)),
            pl.BlockSpec((8, d), lambda i: (jnp.minimum((i + 1) * blk8, last8), 0)),
            pl.BlockSpec((1, N_MOD, d), lambda i: (_mod_row(i, tiles_per_sample), 0, 0)),
            _const_spec((1, d)),
            _const_spec((6, d)),
            _const_spec((3, d, d)),
            _const_spec((d, LANES)),
            _const_spec((2, LANES, d)),
            _const_spec((2, d)),
            _const_spec((d, LANES)),
            _const_spec((2, LANES, d)),
            _const_spec((2, d)),
            _const_spec((d, LANES)),
            _const_spec((LANES, d)),
            _const_spec((1, d)),
            _const_spec((1, d)),
            _const_spec((d, LANES)),
            _const_spec((LANES, d)),
        ],
        out_specs=(row_spec, row_spec, row_spec, row_spec, dir_spec, dir_spec, dir_spec),
        compiler_params=_cparams("arbitrary"),
        name="rwkv_inputs",
    )(x2, x2, x2, mod, gain, p["mu"], p["w_rkv"], p["w1"], p["w2"], p["w0"], p["a1"], p["a2"], p["a0"],
      p["g1"], p["g2"], p["k_k"], p["k_a"], seg, segt)


def _scan_consts(reverse):
    sgn = -1 if reverse else 1
    n2 = 2 * CHUNK
    row = lax.broadcasted_iota(jnp.int32, (n2, n2), 0)
    col = lax.broadcasted_iota(jnp.int32, (n2, n2), 1)
    same = (row // CHUNK) == (col // CHUNK)
    dt = (row % CHUNK - col % CHUNK) * sgn
    rc = lax.broadcasted_iota(jnp.int32, (CHUNK, CHUNK), 0)
    cc = lax.broadcasted_iota(jnp.int32, (CHUNK, CHUNK), 1)
    return {
        "same": same,
        "strict": jnp.logical_and(same, dt > 0),
        "incl": jnp.logical_and(same, dt >= 0),
        "eye": (row == col).astype(F32),
        "tri": ((rc - cc) * sgn >= 0).astype(BF16),
        "head0": lax.broadcasted_iota(jnp.int32, (CHUNK, LANES), 1) < HEAD_DIM,
    }


def _scan_units(units):
    n2 = 2 * CHUNK
    every = lambda f: [f(u) for u in units]

    def stack(u, x):
        return jnp.concatenate([jnp.where(u["c"]["head0"], x, 0.0), jnp.where(u["c"]["head0"], 0.0, x)], axis=0)

    def prep(u):
        cl = _dot_exact_rhs_left(u["c"]["tri"], u["lw"])
        tot = jnp.sum(u["lw"], axis=0, keepdims=True)
        e_ncl = jnp.exp(-cl)
        e_end = jnp.exp(tot - cl)
        u["q2"] = jnp.concatenate([stack(u, u["kk"] * jnp.exp(cl - u["lw"])), stack(u, u["r"] * jnp.exp(cl))], axis=0)
        u["k2"] = jnp.concatenate([stack(u, u["kd"] * e_ncl), stack(u, u["b"] * e_ncl)], axis=0)
        u["ket"] = jnp.concatenate([u["kd"] * e_end, -(u["b"] * e_end)], axis=0).T
        u["g_col"] = jnp.sum(jnp.where(u["c"]["eye"] > 0, jnp.exp(tot), 0.0), axis=1, keepdims=True)
        u["vs"] = stack(u, u["v"])

    every(prep)
    a_all = every(lambda u: _mm(u["q2"], u["k2"], nt=True))
    qm = every(lambda u: _mm(u["q2"], u["m0"]))
    for u, a in zip(units, a_all):
        c = u["c"]
        u["l_kk"] = jnp.where(c["strict"], a[:n2, :n2], 0.0)
        u["l_rk"] = jnp.where(c["incl"], a[n2:, :n2], 0.0)
        u["l_rb"] = jnp.where(c["incl"], a[n2:, n2:], 0.0)
        u["pw"] = -jnp.where(c["strict"], a[:n2, n2:], 0.0)
        u["inv"] = c["eye"] + u["pw"]
    lv = every(lambda u: _mm(u["l_kk"], u["vs"]))
    levels = CHUNK.bit_length() - 1
    sq = every(lambda u: _mm(u["pw"], u["pw"]))
    for u, x in zip(units, sq):
        u["pw"] = x
    for _ in range(1, levels - 1):
        st = every(lambda u: _mm(jnp.concatenate([u["pw"], u["inv"]], axis=0), u["pw"]))
        for u, x in zip(units, st):
            u["pw"] = x[:n2]
            u["inv"] = u["inv"] + x[n2:]
    last = every(lambda u: _mm(u["inv"], u["pw"]))
    for u, x in zip(units, last):
        u["inv"] = u["inv"] + x
    us = [_mm(u["inv"], q[:n2] + t) for u, q, t in zip(units, qm, lv)]
    ys = [q[n2:] + _mm(jnp.concatenate([u["l_rk"], -u["l_rb"]], axis=1), jnp.concatenate([u["vs"], s], axis=0))
          for u, q, s in zip(units, qm, us)]
    new = [_mm(u["ket"], jnp.concatenate([u["v"], s[:CHUNK] + s[CHUNK:]], axis=0)) for u, s in zip(units, us)]
    ms = [u["m0"] * u["g_col"] + jnp.where(u["c"]["same"], x, 0.0) for u, x in zip(units, new)]
    return [y[:CHUNK] + y[CHUNK:] for y in ys], ms


def _dot_exact_rhs_left(a_bf16, b):
    h, m, l = _split3(b)
    return _dot(a_bf16, h) + (_dot(a_bf16, m) + _dot(a_bf16, l))


def _scan_kernel(rf_ref, vf_ref, kkf_ref, rr_ref, vr_ref, kkr_ref, lwf_ref, kdf_ref, bf_ref, lwr_ref, kdr_ref,
                 br_ref, yf_ref, yr_ref, m_ref, *, pairs):
    @pl.when(pl.program_id(2) == 0)
    def _():
        m_ref[...] = jnp.zeros_like(m_ref)

    dirs = (
        (_scan_consts(False), rf_ref, vf_ref, kkf_ref, lwf_ref, kdf_ref, bf_ref, yf_ref),
        (_scan_consts(True), rr_ref, vr_ref, kkr_ref, lwr_ref, kdr_ref, br_ref, yr_ref),
    )
    units = []
    for z, (consts, r_ref, v_ref, kk_ref, lw_ref, kd_ref, b_ref, _) in enumerate(dirs):
        for p in range(pairs):
            sl = slice(p * LANES, (p + 1) * LANES)
            units.append({"c": consts, "r": r_ref[:, sl], "v": v_ref[:, sl], "kk": kk_ref[:, sl], "lw": lw_ref[0, :, sl],
                          "kd": kd_ref[0, :, sl], "b": b_ref[0, :, sl], "m0": m_ref[z * pairs + p]})
    ys, ms = _scan_units(units)
    for i, (y, m1) in enumerate(zip(ys, ms)):
        z, p = divmod(i, pairs)
        dirs[z][-1][0, :, p * LANES:(p + 1) * LANES] = y
        m_ref[i] = m1


def _rwkv_scan(r, v, kk, lw, kd, b, batch, ctx_len, pairs=8):
    n, d = r.shape
    tt = n // batch
    nch = tt // CHUNK
    nch_ctx = ctx_len // CHUNK
    width = pairs * LANES
    groups = d // width

    fwd_row = lambda bb, c: bb * nch + c
    rev_row = lambda bb, c: bb * nch + jnp.where(c < nch_ctx, nch_ctx - 1 - c, nch + nch_ctx - 1 - c)
    shared = lambda row: pl.BlockSpec((CHUNK, width), lambda bb, g, c: (row(bb, c), g))
    per_dir = lambda z, row: pl.BlockSpec((1, CHUNK, width), lambda bb, g, c: (z, row(bb, c), g))
    return pl.pallas_call(
        functools.partial(_scan_kernel, pairs=pairs),
        out_shape=(jax.ShapeDtypeStruct((1, n, d), F32), jax.ShapeDtypeStruct((1, n, d), F32)),
        grid=(batch, groups, nch),
        in_specs=[shared(fwd_row)] * 3 + [shared(rev_row)] * 3 + [per_dir(0, fwd_row)] * 3 + [per_dir(1, rev_row)] * 3,
        out_specs=(per_dir(0, fwd_row), per_dir(0, rev_row)),
        scratch_shapes=[pltpu.VMEM((2 * pairs, LANES, LANES), F32)],
        compiler_params=_cparams("arbitrary", "arbitrary", "arbitrary"),
        name="rwkv_scan",
    )(r, v, kk, r, v, kk, lw, kd, b, lw, kd, b)


def _rwkv_out_kernel(y0_ref, y1_ref, r_ref, kd0_ref, kd1_ref, v_ref, g_ref, x_ref, m_ref, rk_ref, lnw_ref,
                     lnb_ref, wo_ref, seg_ref, segt_ref, o_ref):
    seg, segt = seg_ref[...], segt_ref[...]
    y = y0_ref[0] + y1_ref[0]
    mean = _seg_sum(y, seg, segt) * (1.0 / HEAD_DIM)
    dy = y - mean
    var = _seg_sum(dy * dy, seg, segt) * (1.0 / HEAD_DIM)
    yn = (dy * lax.rsqrt(var + GN_EPS)) * lnw_ref[...] + lnb_ref[...]
    bonus = _seg_sum(r_ref[...] * (kd0_ref[0] + kd1_ref[0]) * rk_ref[...], seg, segt)
    out = (yn + bonus * v_ref[...]) * g_ref[...]
    o = _dot(out.astype(BF16), wo_ref[...])
    o_ref[...] = x_ref[...] + m_ref[0][2:3] * o


def _rwkv_readout(yf, yr, r, kd, v, g, x2, mod, p, seg, segt, tiles_per_sample):
    n, d = x2.shape
    row_spec = pl.BlockSpec((TM, d), lambda i: (i, 0))
    return pl.pallas_call(
        _rwkv_out_kernel,
        out_shape=jax.ShapeDtypeStruct((n, d), F32),
        grid=(n // TM,),
        in_specs=[
            pl.BlockSpec((1, TM, d), lambda i: (0, i, 0)),
            pl.BlockSpec((1, TM, d), lambda i: (0, i, 0)),
            row_spec,
            pl.BlockSpec((1, TM, d), lambda i: (0, i, 0)),
            pl.BlockSpec((1, TM, d), lambda i: (1, i, 0)),
            row_spec, row_spec, row_spec,
            pl.BlockSpec((1, N_MOD, d), lambda i: (_mod_row(i, tiles_per_sample), 0, 0)),
            _const_spec((1, d)), _const_spec((1, d)), _const_spec((1, d)),
            _const_spec((d, d)),
            _const_spec((d, LANES)), _const_spec((LANES, d)),
        ],
        out_specs=row_spec,
        compiler_params=_cparams("arbitrary"),
        name="rwkv_readout",
    )(yf, yr, r, kd, kd, v, g, x2, mod, p["r_k"], p["ln_w"], p["ln_b"], p["w_o"], seg, segt)


def _rope(x, cos, sin_lo, sin_hi):
    w = x.shape[1]
    reps = w // LANES
    tile = lambda t: jnp.concatenate([t] * reps, axis=1) if reps > 1 else t
    half = HEAD_DIM // 4
    return x * tile(cos) + pltpu.roll(x, w - half, 1) * tile(sin_lo) + pltpu.roll(x, half, 1) * tile(sin_hi)


def _gqa_proj_kernel(x_ref, m_ref, gain_ref, w_ref, qg_ref, kg_ref, cos_ref, slo_ref, shi_ref, seg_ref, segt_ref,
                     segk_ref, segtk_ref, q_o, k_o, v_o, *, d, dkv):
    m = m_ref[0]
    h = _norm_mod(x_ref[...], gain_ref[...], m[0:1], m[1:2]).astype(BF16)
    qkv = _dot(h, w_ref[...])
    q, k, v = qkv[:, :d], qkv[:, d:d + dkv], qkv[:, d + dkv:]
    cos, slo, shi = cos_ref[...], slo_ref[...], shi_ref[...]
    qms = _seg_sum(q * q, seg_ref[...], segt_ref[...]) * (1.0 / HEAD_DIM)
    q = (q * lax.rsqrt(qms + NORM_EPS)) * qg_ref[...]
    kms = _seg_sum(k * k, segk_ref[...], segtk_ref[...]) * (1.0 / HEAD_DIM)
    k = (k * lax.rsqrt(kms + NORM_EPS)) * kg_ref[...]
    q = (_rope(q, cos, slo, shi) * (HEAD_DIM ** -0.5 * LOG2_E)).astype(BF16)
    k = _rope(k, cos, slo, shi).astype(BF16)
    v = v.astype(BF16)
    lane = lax.broadcasted_iota(jnp.int32, (v.shape[0], LANES - HEAD_DIM), 1)
    ones_pad = jnp.where(lane == 0, 1.0, 0.0).astype(BF16)
    for hh in range(d // HEAD_DIM):
        q_o[hh] = q[:, hh * HEAD_DIM:(hh + 1) * HEAD_DIM]
    for hh in range(dkv // HEAD_DIM):
        k_o[hh] = k[:, hh * HEAD_DIM:(hh + 1) * HEAD_DIM]
        v_o[hh] = jnp.concatenate([v[:, hh * HEAD_DIM:(hh + 1) * HEAD_DIM], ones_pad], axis=1)


def _gqa_project(x2, mod, gain, p, rope, seg, segt, segk, segtk, tiles_per_sample):
    n, d = x2.shape
    dkv = ATTN_KV_HEADS * HEAD_DIM
    nq, nkv = d // HEAD_DIM, ATTN_KV_HEADS
    heads = lambda h: pl.BlockSpec((h, TM, HEAD_DIM), lambda i: (0, i, 0))
    pos = pl.BlockSpec((TM, LANES), lambda i: (i % tiles_per_sample, 0))
    return pl.pallas_call(
        functools.partial(_gqa_proj_kernel, d=d, dkv=dkv),
        out_shape=(jax.ShapeDtypeStruct((nq, n, HEAD_DIM), BF16), jax.ShapeDtypeStruct((nkv, n, HEAD_DIM), BF16),
                   jax.ShapeDtypeStruct((nkv, n, LANES), BF16)),
        grid=(n // TM,),
        in_specs=[
            pl.BlockSpec((TM, d), lambda i: (i, 0)),
            pl.BlockSpec((1, N_MOD, d), lambda i: (_mod_row(i, tiles_per_sample), 0, 0)),
            _const_spec((1, d)),
            _const_spec((d, d + 2 * dkv)),
            _const_spec((1, d)), _const_spec((1, dkv)),
            pos, pos, pos,
            _const_spec((d, LANES)), _const_spec((LANES, d)),
            _const_spec((dkv, LANES)), _const_spec((LANES, dkv)),
        ],
        out_specs=(heads(nq), heads(nkv), pl.BlockSpec((nkv, TM, LANES), lambda i: (0, i, 0))),
        compiler_params=_cparams("arbitrary"),
        name="gqa_project",
    )(x2, mod, gain, p["w_qkv"], p["q_gain"], p["k_gain"], rope[0], rope[1], rope[2], seg, segt, segk, segtk)


def _attn_kernel(q_ref, k_ref, v_ref, o_ref, *, group, kvs, ctx_rows):
    def attend(kv_rows):
        for kv in range(kvs):
            k = k_ref[kv, :kv_rows, :]
            v = v_ref[kv, :kv_rows, :]
            for hh in range(kv * group, (kv + 1) * group):
                s = _dot_nt(q_ref[hh], k)
                pr = jnp.exp2(s - jnp.max(s, axis=-1, keepdims=True))
                oe = _dot(pr.astype(BF16), v)
                o_ref[hh] = (oe[:, :HEAD_DIM] / oe[:, HEAD_DIM:HEAD_DIM + 1]).astype(BF16)

    is_ctx = pl.program_id(2) == 0
    pl.when(is_ctx)(lambda: attend(ctx_rows))
    pl.when(jnp.logical_not(is_ctx))(lambda: attend(k_ref.shape[1]))


def _attention(q, k, v, batch, tiles_per_sample, ctx_rows, kvs=2):
    nq, n, _ = q.shape
    nkv = k.shape[0]
    group = nq // nkv
    tt = n // batch
    q_spec = pl.BlockSpec((kvs * group, TM, HEAD_DIM), lambda bb, g, t: (g, bb * tiles_per_sample + t, 0))
    return pl.pallas_call(
        functools.partial(_attn_kernel, group=group, kvs=kvs, ctx_rows=ctx_rows),
        out_shape=jax.ShapeDtypeStruct((nq, n, HEAD_DIM), BF16),
        grid=(batch, nkv // kvs, tiles_per_sample),
        in_specs=[q_spec, pl.BlockSpec((kvs, tt, HEAD_DIM), lambda bb, g, t: (g, bb, 0)),
                  pl.BlockSpec((kvs, tt, LANES), lambda bb, g, t: (g, bb, 0))],
        out_specs=q_spec,
        compiler_params=_cparams("arbitrary", "arbitrary", "arbitrary"),
        name="gqa_attention",
    )(q, k, v)


def _attn_out_kernel(o_ref, x_ref, m_ref, wo_ref, y_ref):
    o = jnp.concatenate([o_ref[hh] for hh in range(o_ref.shape[0])], axis=1)
    y_ref[...] = x_ref[...] + m_ref[0][2:3] * _dot(o, wo_ref[...])


def _attn_out(o, x2, mod, w_o, tiles_per_sample):
    n, d = x2.shape
    nq = o.shape[0]
    row_spec = pl.BlockSpec((TM, d), lambda i: (i, 0))
    return pl.pallas_call(
        _attn_out_kernel,
        out_shape=jax.ShapeDtypeStruct((n, d), F32),
        grid=(n // TM,),
        in_specs=[
            pl.BlockSpec((nq, TM, HEAD_DIM), lambda i: (0, i, 0)),
            row_spec,
            pl.BlockSpec((1, N_MOD, d), lambda i: (_mod_row(i, tiles_per_sample), 0, 0)),
            _const_spec((d, d)),
        ],
        out_specs=row_spec,
        compiler_params=_cparams("arbitrary"),
        name="gqa_out_proj",
    )(o, x2, mod, w_o)


def _router_kernel(x_ref, m_ref, gain_ref, wr_ref, f_o, afft_o):
    m = m_ref[0]
    f = _norm_mod(x_ref[...], gain_ref[...], m[3:4], m[4:5])
    f_o[...] = f
    logits = _mm3(f, wr_ref[...])
    lane = lax.broadcasted_iota(jnp.int32, logits.shape, 1)
    logits = jnp.where(lane < N_EXPERTS, logits, -1e30)
    e = jnp.exp(logits - jnp.max(logits, axis=-1, keepdims=True))
    aff = e / jnp.sum(e, axis=-1, keepdims=True)
    afft_o[0] = aff.T[:N_EXPERTS]


def _moe_router(x2, mod, gain, w_router, batch, tiles_per_sample):
    n, d = x2.shape
    tt = n // batch
    return pl.pallas_call(
        _router_kernel,
        out_shape=(jax.ShapeDtypeStruct((n, d), F32), jax.ShapeDtypeStruct((batch, N_EXPERTS, tt), F32)),
        grid=(n // TM,),
        in_specs=[
            pl.BlockSpec((TM, d), lambda i: (i, 0)),
            pl.BlockSpec((1, N_MOD, d), lambda i: (_mod_row(i, tiles_per_sample), 0, 0)),
            _const_spec((1, d)),
            _const_spec((d, LANES)),
        ],
        out_specs=(pl.BlockSpec((TM, d), lambda i: (i, 0)),
                   pl.BlockSpec((1, N_EXPERTS, TM), lambda i: (i // tiles_per_sample, 0, i % tiles_per_sample))),
        compiler_params=_cparams("arbitrary"),
        name="moe_router",
    )(x2, mod, gain, w_router)


def _prefix_excl(mask, tri_excl):
    xb = mask.astype(BF16)
    carry = jnp.zeros((mask.shape[0], 1), F32)
    outs = []
    for blk in range(mask.shape[1] // LANES):
        piece = xb[:, blk * LANES:(blk + 1) * LANES]
        outs.append(_dot(piece, tri_excl) + carry)
        carry = carry + jnp.sum(piece.astype(F32), axis=1, keepdims=True)
    return jnp.concatenate(outs, axis=1) if len(outs) > 1 else outs[0]


def _top_cap(a, cap, tri_excl):
    ai = lax.bitcast_convert_type(a, jnp.int32)

    def body(i, thr):
        cand = thr | jnp.left_shift(jnp.int32(1), F32_VALUE_BITS - 1 - i)
        cnt = jnp.sum((ai >= cand).astype(F32), axis=1, keepdims=True)
        return jnp.where(cnt >= cap, cand, thr)

    thr = lax.fori_loop(0, F32_VALUE_BITS, body, jnp.zeros((a.shape[0], 1), jnp.int32))
    gt = ai > thr
    eq = ai == thr
    need = cap - jnp.sum(gt.astype(F32), axis=1, keepdims=True)
    sel = jnp.logical_or(gt, jnp.logical_and(eq, _prefix_excl(eq, tri_excl) < need))
    return sel, _prefix_excl(sel, tri_excl)


def _select_kernel(afft_ref, tri_ref, idx_o, gate_o, sel_s, pos_s, *, sets):
    e = pl.program_id(1)

    @pl.when(e == 0)
    def _():
        for off, n, cap, _ in sets:
            sel, pos = _top_cap(afft_ref[0, :, off:off + n], cap, tri_ref[...])
            sel_s[:, off:off + n] = sel.astype(F32)
            pos_s[:, off:off + n] = pos

    for off, n, cap, slot0 in sets:
        a_e = afft_ref[0, pl.ds(e, 1), off:off + n]
        sel_e = sel_s[pl.ds(e, 1), off:off + n]
        pos_e = pos_s[pl.ds(e, 1), off:off + n]
        wl = min(cap, LANES)
        nw = cap // wl
        pos_i = pos_e.astype(jnp.int32)
        lane_of = jnp.bitwise_and(pos_i, wl - 1)
        win_of = jnp.right_shift(pos_i, wl.bit_length() - 1)
        lane_id = lax.broadcasted_iota(jnp.int32, (wl, n), 0)
        onehot = jnp.where(jnp.logical_and(lane_of == lane_id, sel_e > 0), 1.0, 0.0).astype(BF16)
        tok = lax.broadcasted_iota(jnp.int32, (BF16_ROWS, n), 1)
        rid = lax.broadcasted_iota(jnp.int32, (BF16_ROWS, n), 0)
        a_h, a_m, a_l = _split3(a_e)
        rows = jnp.where(rid == 0, jnp.right_shift(tok, TOK_RADIX.bit_length() - 1).astype(F32),
               jnp.where(rid == 1, jnp.bitwise_and(tok, TOK_RADIX - 1).astype(F32),
               jnp.where(rid == 2, a_h.astype(F32),
               jnp.where(rid == 3, a_m.astype(F32),
               jnp.where(rid == 4, a_l.astype(F32), 0.0)))))
        stacked = jnp.concatenate([jnp.where(win_of == wi, rows, 0.0) for wi in range(nw)], axis=0).astype(BF16)
        res_all = _dot_nt(stacked, onehot)
        for wi in range(nw):
            res = res_all[BF16_ROWS * wi:BF16_ROWS * (wi + 1)]
            lo = slot0 + wi * wl
            idx_o[0, 0, :, lo:lo + wl] = (res[0:1] * TOK_RADIX + res[1:2]).astype(jnp.int32) + off
            gate_o[0, 0, :, lo:lo + wl] = res[2:3] + (res[3:4] + res[4:5])


def _moe_select(afft, sets, n_slots):
    batch, ne, tt = afft.shape
    tri = (jnp.arange(LANES)[:, None] < jnp.arange(LANES)[None, :]).astype(BF16)
    slot_spec = pl.BlockSpec((1, 1, 1, n_slots), lambda bb, e: (bb, e, 0, 0))
    return pl.pallas_call(
        functools.partial(_select_kernel, sets=sets),
        out_shape=(jax.ShapeDtypeStruct((batch, ne, 1, n_slots), jnp.int32),
                   jax.ShapeDtypeStruct((batch, ne, 1, n_slots), F32)),
        grid=(batch, ne),
        in_specs=[pl.BlockSpec((1, ne, tt), lambda bb, e: (bb, 0, 0)), _const_spec((LANES, LANES))],
        out_specs=(slot_spec, slot_spec),
        scratch_shapes=[pltpu.VMEM((ne, tt), F32), pltpu.VMEM((ne, tt), F32)],
        compiler_params=_cparams("arbitrary", "arbitrary"),
        name="moe_select",
    )(afft, tri)


def _gather_kernel(idx_ref, f_ref, xe_o, buf, *, n_slots):
    def body(c, _):
        r = idx_ref[0, 0, 0, c]
        buf[pl.ds(c, 1), :] = f_ref[pl.ds(r, 1), :]
        return 0

    lax.fori_loop(0, n_slots, body, 0, unroll=8)
    xe_o[0, 0] = buf[...].astype(BF16)


def _moe_gather(idx, f, batch):
    n, d = f.shape
    tt = n // batch
    ne, n_slots = idx.shape[1], idx.shape[3]
    return pl.pallas_call(
        functools.partial(_gather_kernel, n_slots=n_slots),
        out_shape=jax.ShapeDtypeStruct((batch, ne, n_slots, d), BF16),
        grid=(batch, ne),
        in_specs=[
            pl.BlockSpec((1, 1, 1, n_slots), lambda bb, e: (bb, e, 0, 0), memory_space=pltpu.SMEM),
            pl.BlockSpec((tt, d), lambda bb, e: (bb, 0)),
        ],
        out_specs=pl.BlockSpec((1, 1, n_slots, d), lambda bb, e: (bb, e, 0, 0)),
        scratch_shapes=[pltpu.VMEM((n_slots, d), F32)],
        compiler_params=_cparams("arbitrary", "arbitrary"),
        name="moe_gather",
    )(idx, f)


def _expert_kernel(xe_ref, wg_ref, wu_ref, wd_ref, ml_ref, mc_ref, ye_o, *, cap_l):
    first = pl.program_id(2) == 0
    last = pl.program_id(2) == pl.num_programs(2) - 1
    wg = wg_ref[0, 0].astype(BF16)
    wu = wu_ref[0, 0].astype(BF16)
    wd = wd_ref[0, 0].astype(BF16)
    for i in range(xe_ref.shape[0]):
        x = xe_ref[i, 0]
        h = (_silu(_dot(x, wg)) * _dot(x, wu)).astype(BF16)
        y = _dot(h, wd)
        ye_o[i, 0] = jnp.where(first, y, ye_o[i, 0] + y)

    @pl.when(last)
    def _():
        latent_slot = lax.broadcasted_iota(jnp.int32, ye_o.shape[2:], 0) < cap_l
        for i in range(xe_ref.shape[0]):
            ye_o[i, 0] = ye_o[i, 0] * jnp.where(latent_slot, ml_ref[i], mc_ref[...])


def _moe_experts(xe, w_gate_up, w_down, gate_lat, gate_ctx, layer, cap_l, halves=2, tf=256):
    batch, ne, s, d = xe.shape
    de = w_down.shape[2]
    nf = de // tf
    bh = batch // halves
    tok = pl.BlockSpec((bh, 1, s, d), lambda e, mh, f: (mh, e, 0, 0))
    return pl.pallas_call(
        functools.partial(_expert_kernel, cap_l=cap_l),
        out_shape=jax.ShapeDtypeStruct((batch, ne, s, d), F32),
        grid=(ne, halves, nf),
        in_specs=[
            tok,
            pl.BlockSpec((1, 1, d, tf), lambda e, mh, f: (layer, e, 0, f)),
            pl.BlockSpec((1, 1, d, tf), lambda e, mh, f: (layer, e, 0, nf + f)),
            pl.BlockSpec((1, 1, tf, d), lambda e, mh, f: (layer, e, f, 0)),
            pl.BlockSpec((bh, 1, d), lambda e, mh, f: (mh, 0, 0)),
            _const_spec((1, d)),
        ],
        out_specs=tok,
        compiler_params=_cparams("arbitrary", "arbitrary", "arbitrary"),
        name="moe_experts",
    )(xe, w_gate_up, w_gate_up, w_down, gate_lat, gate_ctx)


def _combine_kernel(idx_ref, gate_ref, ye_ref, x_hbm, out_o, sem, *, n_slots):
    bb = pl.program_id(0)
    rows = out_o.shape[0]

    @pl.when(pl.program_id(1) == 0)
    def _():
        seed = pltpu.make_async_copy(x_hbm.at[pl.ds(bb * rows, rows)], out_o, sem)
        seed.start()
        seed.wait()

    def body(g, _):
        base = pl.multiple_of(g * SCATTER_GROUP, SCATTER_GROUP)
        rs = [idx_ref[0, 0, 0, base + j] for j in range(SCATTER_GROUP)]
        rows = ye_ref[0, 0, pl.ds(base, SCATTER_GROUP), :]
        new = [out_o[pl.ds(r, 1), :] + gate_ref[0, 0, 0, base + j] * rows[j:j + 1, :] for j, r in enumerate(rs)]
        for r, v in zip(rs, new):
            out_o[pl.ds(r, 1), :] = v
        return 0

    assert n_slots % SCATTER_GROUP == 0
    lax.fori_loop(0, n_slots // SCATTER_GROUP, body, 0)


def _moe_combine(idx, gate, ye, x2, tt):
    batch, ne, n_slots, d = ye.shape
    smem = pl.BlockSpec((1, 1, 1, n_slots), lambda bb, e: (bb, e, 0, 0), memory_space=pltpu.SMEM)
    return pl.pallas_call(
        functools.partial(_combine_kernel, n_slots=n_slots),
        out_shape=jax.ShapeDtypeStruct((batch * tt, d), F32),
        grid=(batch, ne),
        in_specs=[smem, smem, pl.BlockSpec((1, 1, n_slots, d), lambda bb, e: (bb, e, 0, 0)),
                  pl.BlockSpec(memory_space=pl.ANY)],
        out_specs=pl.BlockSpec((tt, d), lambda bb, e: (bb, 0)),
        scratch_shapes=[pltpu.SemaphoreType.DMA(())],
        compiler_params=_cparams("arbitrary", "arbitrary"),
        name="moe_combine",
    )(idx, gate, ye, x2)


def _final_kernel(x_ref, g_ref, o_ref):
    x = x_ref[...]
    ms = jnp.mean(x * x, axis=-1, keepdims=True)
    o_ref[0] = (x * lax.rsqrt(ms + NORM_EPS)) * g_ref[...]


def _final_norm(x2, gain, batch, tiles_per_sample):
    n, d = x2.shape
    lat_tiles = tiles_per_sample - 1
    return pl.pallas_call(
        _final_kernel,
        out_shape=jax.ShapeDtypeStruct((batch, lat_tiles * TM, d), F32),
        grid=(batch, lat_tiles),
        in_specs=[pl.BlockSpec((TM, d), lambda bb, t: (bb * tiles_per_sample + 1 + t, 0)), _const_spec((1, d))],
        out_specs=pl.BlockSpec((1, TM, d), lambda bb, t: (bb, t, 0)),
        compiler_params=_cparams("arbitrary", "arbitrary"),
        name="final_norm",
    )(x2, gain)


def _segment_matrices(width):
    heads = jnp.arange(width) // HEAD_DIM
    seg = (heads[:, None] == jnp.arange(LANES)[None, :]).astype(BF16)
    return seg, seg.T


def _rope_tables(seq, ctx_len):
    t = jnp.arange(seq)
    pos = jnp.stack([(t // GRID_W).astype(F32), (t % GRID_W).astype(F32)], axis=-1)
    half = HEAD_DIM // 4
    inv_freq = ROPE_THETA ** (-jnp.arange(0, 2 * half, 2, dtype=F32) / (2 * half))
    ang = pos[:, :, None] * inv_freq
    cos, sin = jnp.cos(ang), jnp.sin(ang)
    zero = jnp.zeros_like(sin)
    per_head = lambda first, second: jnp.concatenate([first, second], axis=-1).reshape(seq, HEAD_DIM)
    cos_h = per_head(cos, cos)
    sin_lo = per_head(-sin, zero)
    sin_hi = per_head(zero, sin)
    pad = lambda tab, fill: jnp.concatenate([jnp.full((ctx_len, HEAD_DIM), fill, F32), tab], axis=0)
    two = lambda tab: jnp.concatenate([tab, tab], axis=1)
    return two(pad(cos_h, 1.0)), two(pad(sin_lo, 0.0)), two(pad(sin_hi, 0.0))


def kernel(x, c, ctx, c_ctx, mod_w, mod_b, norm_mix, norm_ffn, rwkv_mu, rwkv_w_rkv, rwkv_w0, rwkv_w1, rwkv_w2,
           rwkv_a0, rwkv_a1, rwkv_a2, rwkv_g1, rwkv_g2, rwkv_k_k, rwkv_k_a, rwkv_r_k, rwkv_ln_w, rwkv_ln_b,
           rwkv_w_o, attn_w_qkv, attn_q_gain, attn_k_gain, attn_w_o, moe_router, moe_w_gate_up, moe_w_down,
           final_norm):
    batch, seq, d = x.shape
    ctx_len = ctx.shape[1]
    depth = mod_w.shape[0]
    tt = ctx_len + seq
    tps = tt // TM
    assert ctx_len == TM and seq % TM == 0 and CHUNK == HEAD_DIM and batch + 1 <= 16
    nq = d // HEAD_DIM

    x2 = jnp.concatenate([ctx, x], axis=1).reshape(batch * tt, d)
    cc = jnp.zeros((16, d), F32).at[0].set(c_ctx).at[1:batch + 1].set(c)
    mods = _mod_vectors(cc, mod_w, mod_b).reshape(depth, 16, N_MOD, d)
    seg, segt = _segment_matrices(d)
    segk, segtk = _segment_matrices(ATTN_KV_HEADS * HEAD_DIM)
    rope = _rope_tables(seq, ctx_len)
    cap_l = EC_CAPACITY * seq // N_EXPERTS
    cap_c = EC_CAPACITY * ctx_len // N_EXPERTS
    sets = ((ctx_len, seq, cap_l, 0), (0, ctx_len, cap_c, cap_l))
    router_pad = jnp.zeros((depth, d, LANES), F32).at[:, :, :N_EXPERTS].set(moe_router)

    ia = ib = 0
    for i in range(depth):
        mod = mods[i]
        if i % 2 == 0:
            zpad = jnp.zeros((HEAD_DIM, d), F32)
            lora_pad = lambda w: jnp.stack([jnp.concatenate([w[0], zpad], 0), jnp.concatenate([zpad, w[1]], 0)]).astype(BF16)
            p = {
                "mu": rwkv_mu[ia], "w_rkv": rwkv_w_rkv[ia].astype(BF16),
                "w1": jnp.concatenate([rwkv_w1[ia, 0], rwkv_w1[ia, 1]], axis=1).astype(BF16), "w2": lora_pad(rwkv_w2[ia]),
                "w0": rwkv_w0[ia],
                "a1": jnp.concatenate([rwkv_a1[ia, 0], rwkv_a1[ia, 1]], axis=1).astype(BF16), "a2": lora_pad(rwkv_a2[ia]),
                "a0": rwkv_a0[ia],
                "g1": rwkv_g1[ia].astype(BF16), "g2": rwkv_g2[ia].astype(BF16),
                "k_k": rwkv_k_k[ia].reshape(1, d), "k_a": rwkv_k_a[ia].reshape(1, d),
                "r_k": rwkv_r_k[ia].reshape(1, d), "ln_w": rwkv_ln_w[ia].reshape(1, d), "ln_b": rwkv_ln_b[ia].reshape(1, d),
                "w_o": rwkv_w_o[ia].astype(BF16),
            }
            r, v, kk, g, lw, kd, b = _rwkv_inputs(x2, mod, norm_mix[i].reshape(1, d), p, seg, segt, tps)
            yf, yr = _rwkv_scan(r, v, kk, lw, kd, b, batch, ctx_len)
            x2 = _rwkv_readout(yf, yr, r, kd, v, g, x2, mod, p, seg, segt, tps)
            ia += 1
        else:
            p = {
                "w_qkv": attn_w_qkv[ib].astype(BF16),
                "q_gain": jnp.tile(attn_q_gain[ib], nq).reshape(1, d),
                "k_gain": jnp.tile(attn_k_gain[ib], ATTN_KV_HEADS).reshape(1, ATTN_KV_HEADS * HEAD_DIM),
            }
            q, k, v = _gqa_project(x2, mod, norm_mix[i].reshape(1, d), p, rope, seg, segt, segk, segtk, tps)
            o = _attention(q, k, v, batch, tps, ctx_len)
            x2 = _attn_out(o, x2, mod, attn_w_o[ib].astype(BF16), tps)
            ib += 1
        f, afft = _moe_router(x2, mod, norm_ffn[i].reshape(1, d), router_pad[i], batch, tps)
        layer_sets = sets if i < depth - 1 else sets[:1]
        idx, gate = _moe_select(afft, layer_sets, sum(s[2] for s in layer_sets))
        xe = _moe_gather(idx, f, batch)
        ye = _moe_experts(xe, moe_w_gate_up, moe_w_down, mod[1:batch + 1, 5:6], mod[0, 5:6], i, cap_l)
        x2 = _moe_combine(idx, gate, ye, x2, tt)
    return _final_norm(x2, final_norm.reshape(1, d), batch, tps)
```

```python
import functools

import jax
import jax.numpy as jnp
from jax import lax
from jax.experimental import pallas as pl
from jax.experimental.pallas import tpu as pltpu

F32 = jnp.float32
BF16 = jnp.bfloat16

HEAD_DIM = 64
N_MOD = 6
NORM_EPS = 1e-6
GN_EPS = 64e-5
ROPE_THETA = 10000.0
GRID_W = 64
ATTN_KV_HEADS = 4
N_EXPERTS = 16
EC_CAPACITY = 2
TM = 256
CHUNK = 64
LANES = 128
VMEM_LIMIT = 56 * 1024 * 1024
NEG_EXP_M05 = -0.6065306597126334
LOG2_E = 1.4426950408889634
F32_VALUE_BITS = 31
BF16_ROWS = 16
TOK_RADIX = 64
SCATTER_GROUP = 8


def _cparams(*sem):
    return pltpu.CompilerParams(dimension_semantics=sem, vmem_limit_bytes=VMEM_LIMIT)


def _split2(x):
    hi = x.astype(BF16)
    lo = (x - hi.astype(F32)).astype(BF16)
    return hi, lo


def _split3(x):
    hi = x.astype(BF16)
    r1 = x - hi.astype(F32)
    mid = r1.astype(BF16)
    lo = (r1 - mid.astype(F32)).astype(BF16)
    return hi, mid, lo


def _dot(a, b):
    return jnp.dot(a, b, preferred_element_type=F32)


def _dot_nt(a, b):
    return lax.dot_general(a, b, (((1,), (1,)), ((), ())), preferred_element_type=F32)


def _mm(a, b, nt=False):
    return (_dot_nt if nt else _dot)(a.astype(BF16), b.astype(BF16))


def _mm3(a, b):
    ah, al = _split2(a)
    bh, bl = _split2(b)
    return _dot(ah, bh) + (_dot(ah, bl) + _dot(al, bh))


def _dot_exact_rhs(a, b_bf16):
    h, l = _split2(a)
    return _dot(h, b_bf16) + _dot(l, b_bf16)


def _seg_sum(x, seg, segt):
    s = _dot_exact_rhs(x, seg)
    return _dot_exact_rhs(s, segt)


def _norm_mod(x, gain, shift, scale):
    ms = jnp.mean(x * x, axis=-1, keepdims=True)
    y = x * lax.rsqrt(ms + NORM_EPS)
    return (y * gain) * (1.0 + scale) + shift


def _sigmoid(x):
    return 0.5 * jnp.tanh(0.5 * x) + 0.5


def _silu(x):
    return x * _sigmoid(x)


def _mod_row(i, tiles_per_sample):
    return jnp.where(i % tiles_per_sample == 0, 0, 1 + i // tiles_per_sample)


def _const_spec(shape):
    nd = len(shape)
    return pl.BlockSpec(shape, lambda *_: (0,) * nd)


def _mod_kernel(c_ref, w_ref, b_ref, o_ref):
    s = _silu(c_ref[...])
    o_ref[0] = _mm3(s, w_ref[0]) + b_ref[0]


def _mod_vectors(cc, mod_w, mod_b):
    depth, d, n = mod_w.shape
    tn = 1536
    return pl.pallas_call(
        _mod_kernel,
        out_shape=jax.ShapeDtypeStruct((depth, 16, n), F32),
        grid=(depth, n // tn),
        in_specs=[
            pl.BlockSpec((16, d), lambda l, j: (0, 0)),
            pl.BlockSpec((1, d, tn), lambda l, j: (l, 0, j)),
            pl.BlockSpec((1, 1, tn), lambda l, j: (l, 0, j)),
        ],
        out_specs=pl.BlockSpec((1, 16, tn), lambda l, j: (l, 0, j)),
        compiler_params=_cparams("arbitrary", "arbitrary"),
        name="mod_vectors",
    )(cc, mod_w, mod_b.reshape(depth, 1, n))


def _rwkv_in_kernel(x_ref, xp_ref, xn_ref, m_ref, gain_ref, mu_ref, wrkv_ref, w1_ref, w2_ref, w0_ref,
                    a1_ref, a2_ref, a0_ref, g1_ref, g2_ref, kk_ref_, ka_ref, seg_ref, segt_ref,
                    r_o, v_o, kk_o, g_o, lw_o, kd_o, b_o, *, tiles_per_sample):
    i = pl.program_id(0)
    j = i % tiles_per_sample
    m = m_ref[0]
    shift, scale = m[0:1], m[1:2]
    gain = gain_ref[...]
    h = _norm_mod(x_ref[...], gain, shift, scale)
    has_prev = (j >= 2).astype(F32)
    has_next = jnp.logical_and(j >= 1, j <= tiles_per_sample - 2).astype(F32)
    hp_row = _norm_mod(xp_ref[7:8, :], gain, shift, scale) * has_prev
    hn_row = _norm_mod(xn_ref[0:1, :], gain, shift, scale) * has_next
    row = lax.broadcasted_iota(jnp.int32, h.shape, 0)
    h_prev = jnp.where(row == 0, hp_row, pltpu.roll(h, 1, 0))
    h_next = jnp.where(row == TM - 1, hn_row, pltpu.roll(h, TM - 1, 0))
    xx = 0.5 * (h_prev + h_next) - h
    mu = mu_ref[...]

    def mix(n):
        return (h + xx * mu[n:n + 1]).astype(BF16)

    r = _dot(mix(0), wrkv_ref[0])
    k = _dot(mix(1), wrkv_ref[1])
    v = _dot(mix(2), wrkv_ref[2])
    tw = jnp.tanh(_dot(mix(3), w1_ref[...])).astype(BF16)
    ua = _dot(mix(4), a1_ref[...]).astype(BF16)
    g = _dot(_sigmoid(_dot(mix(5), g1_ref[...])).astype(BF16), g2_ref[...])
    kk = k * kk_ref_[...]
    n2 = _seg_sum(kk * kk, seg_ref[...], segt_ref[...])
    kk = kk / jnp.maximum(jnp.sqrt(n2), 1e-12)
    r_o[...] = r
    v_o[...] = v
    kk_o[...] = kk
    g_o[...] = g
    ka = ka_ref[...]
    for z in range(2):
        w_pre = w0_ref[z:z + 1, :] + _dot(tw, w2_ref[z])
        lw_o[z] = NEG_EXP_M05 * _sigmoid(w_pre)
        a =_sigmoid(a0_ref[z:z + 1, :] + _dot(ua, a2_ref[z]))
        kd_o[z] = k * (1.0 + (a - 1.0) * ka)
        b_o[z] = kk * a


def _rwkv_inputs(x2, mod, gain, p, seg, segt, tiles_per_sample):
    n, d = x2.shape
    nt = n // TM
    blk8 = TM // 8
    last8 = n // 8 - 1
    row_spec = pl.BlockSpec((TM, d), lambda i: (i, 0))
    dir_spec = pl.BlockSpec((2, TM, d), lambda i: (0, i, 0))
    tok = jax.ShapeDtypeStruct((n, d), F32)
    tok2 = jax.ShapeDtypeStruct((2, n, d), F32)
    return pl.pallas_call(
        functools.partial(_rwkv_in_kernel, tiles_per_sample=tiles_per_sample),
        out_shape=(tok, tok, tok, tok, tok2, tok2, tok2),
        grid=(nt,),
        in_specs=[
            row_spec,
            pl.BlockSpec((8, d), lambda i: (jnp.maximum(i * blk8 - 1, 0), 0)),
            pl.BlockSpec((8, d), lambda i: (jnp.minimum((i + 1) * blk8, last8), 0)),
            pl.BlockSpec((1, N_MOD, d), lambda i: (_mod_row(i, tiles_per_sample), 0, 0)),
            _const_spec((1, d)),
            _const_spec((6, d)),
            _const_spec((3, d, d)),
            _const_spec((d, LANES)),
            _const_spec((2, LANES, d)),
            _const_spec((2, d)),
            _const_spec((d, LANES)),
            _const_spec((2, LANES, d)),
            _const_spec((2, d)),
            _const_spec((d, LANES)),
            _const_spec((LANES, d)),
            _const_spec((1, d)),
            _const_spec((1, d)),
            _const_spec((d, LANES)),
            _const_spec((LANES, d)),
        ],
        out_specs=(row_spec, row_spec, row_spec, row_spec, dir_spec, dir_spec, dir_spec),
        compiler_params=_cparams("arbitrary"),
        name="rwkv_inputs",
    )(x2, x2, x2, mod, gain, p["mu"], p["w_rkv"], p["w1"], p["w2"], p["w0"], p["a1"], p["a2"], p["a0"],
      p["g1"], p["g2"], p["k_k"], p["k_a"], seg, segt)


def _scan_consts(reverse):
    sgn = -1 if reverse else 1
    n2 = 2 * CHUNK
    row = lax.broadcasted_iota(jnp.int32, (n2, n2), 0)
    col = lax.broadcasted_iota(jnp.int32, (n2, n2), 1)
    same = (row // CHUNK) == (col // CHUNK)
    dt = (row % CHUNK - col % CHUNK) * sgn
    rc = lax.broadcasted_iota(jnp.int32, (CHUNK, CHUNK), 0)
    cc = lax.broadcasted_iota(jnp.int32, (CHUNK, CHUNK), 1)
    return {
        "same": same,
        "strict": jnp.logical_and(same, dt > 0),
        "incl": jnp.logical_and(same, dt >= 0),
        "eye": (row == col).astype(F32),
        "tri": ((rc - cc) * sgn >= 0).astype(BF16),
        "head0": lax.broadcasted_iota(jnp.int32, (CHUNK, LANES), 1) < HEAD_DIM,
    }


def _scan_units(units):
    n2 = 2 * CHUNK
    every = lambda f: [f(u) for u in units]

    def stack(u, x):
        return jnp.concatenate([jnp.where(u["c"]["head0"], x, 0.0), jnp.where(u["c"]["head0"], 0.0, x)], axis=0)

    def prep(u):
        cl = _dot_exact_rhs_left(u["c"]["tri"], u["lw"])
        tot = jnp.sum(u["lw"], axis=0, keepdims=True)
        e_ncl = jnp.exp(-cl)
        e_end = jnp.exp(tot - cl)
        u["q2"] = jnp.concatenate([stack(u, u["kk"] * jnp.exp(cl - u["lw"])), stack(u, u["r"] * jnp.exp(cl))], axis=0)
        u["k2"] = jnp.concatenate([stack(u, u["kd"] * e_ncl), stack(u, u["b"] * e_ncl)], axis=0)
        u["ket"] = jnp.concatenate([u["kd"] * e_end, -(u["b"] * e_end)], axis=0).T
        u["g_col"] = jnp.sum(jnp.where(u["c"]["eye"] > 0, jnp.exp(tot), 0.0), axis=1, keepdims=True)
        u["vs"] = stack(u, u["v"])

    every(prep)
    a_all = every(lambda u: _mm(u["q2"], u["k2"], nt=True))
    qm = every(lambda u: _mm(u["q2"], u["m0"]))
    for u, a in zip(units, a_all):
        c = u["c"]
        u["l_kk"] = jnp.where(c["strict"], a[:n2, :n2], 0.0)
        u["l_rk"] = jnp.where(c["incl"], a[n2:, :n2], 0.0)
        u["l_rb"] = jnp.where(c["incl"], a[n2:, n2:], 0.0)
        u["pw"] = -jnp.where(c["strict"], a[:n2, n2:], 0.0)
        u["inv"] = c["eye"] + u["pw"]
    lv = every(lambda u: _mm(u["l_kk"], u["vs"]))
    levels = CHUNK.bit_length() - 1
    sq = every(lambda u: _mm(u["pw"], u["pw"]))
    for u, x in zip(units, sq):
        u["pw"] = x
    for _ in range(1, levels - 1):
        st = every(lambda u: _mm(jnp.concatenate([u["pw"], u["inv"]], axis=0), u["pw"]))
        for u, x in zip(units, st):
            u["pw"] = x[:n2]
            u["inv"] = u["inv"] + x[n2:]
    last = every(lambda u: _mm(u["inv"], u["pw"]))
    for u, x in zip(units, last):
        u["inv"] = u["inv"] + x
    us = [_mm(u["inv"], q[:n2] + t) for u, q, t in zip(units, qm, lv)]
    ys = [q[n2:] + _mm(jnp.concatenate([u["l_rk"], -u["l_rb"]], axis=1), jnp.concatenate([u["vs"], s], axis=0))
          for u, q, s in zip(units, qm, us)]
    new = [_mm(u["ket"], jnp.concatenate([u["v"], s[:CHUNK] + s[CHUNK:]], axis=0)) for u, s in zip(units, us)]
    ms = [u["m0"] * u["g_col"] + jnp.where(u["c"]["same"], x, 0.0) for u, x in zip(units, new)]
    return [y[:CHUNK] + y[CHUNK:] for y in ys], ms


def _dot_exact_rhs_left(a_bf16, b):
    h, m, l = _split3(b)
    return _dot(a_bf16, h) + (_dot(a_bf16, m) + _dot(a_bf16, l))


def _scan_kernel(rf_ref, vf_ref, kkf_ref, rr_ref, vr_ref, kkr_ref, lwf_ref, kdf_ref, bf_ref, lwr_ref, kdr_ref,
                 br_ref, yf_ref, yr_ref, m_ref, *, pairs):
    @pl.when(pl.program_id(2) == 0)
    def _():
        m_ref[...] = jnp.zeros_like(m_ref)

    dirs = (
        (_scan_consts(False), rf_ref, vf_ref, kkf_ref, lwf_ref, kdf_ref, bf_ref, yf_ref),
        (_scan_consts(True), rr_ref, vr_ref, kkr_ref, lwr_ref, kdr_ref, br_ref, yr_ref),
    )
    units = []
    for z, (consts, r_ref, v_ref, kk_ref, lw_ref, kd_ref, b_ref, _) in enumerate(dirs):
        for p in range(pairs):
            sl = slice(p * LANES, (p + 1) * LANES)
            units.append({"c": consts, "r": r_ref[:, sl], "v": v_ref[:, sl], "kk": kk_ref[:, sl], "lw": lw_ref[0, :, sl],
                          "kd": kd_ref[0, :, sl], "b": b_ref[0, :, sl], "m0": m_ref[z * pairs + p]})
    ys, ms = _scan_units(units)
    for i, (y, m1) in enumerate(zip(ys, ms)):
        z, p = divmod(i, pairs)
        dirs[z][-1][0, :, p * LANES:(p + 1) * LANES] = y
        m_ref[i] = m1


def _rwkv_scan(r, v, kk, lw, kd, b, batch, ctx_len, pairs=8):
    n, d = r.shape
    tt = n // batch
    nch = tt // CHUNK
    nch_ctx = ctx_len // CHUNK
    width = pairs * LANES
    groups = d // width

    fwd_row = lambda bb, c: bb * nch + c
    rev_row = lambda bb, c: bb * nch + jnp.where(c < nch_ctx, nch_ctx - 1 - c, nch + nch_ctx - 1 - c)
    shared = lambda row: pl.BlockSpec((CHUNK, width), lambda bb, g, c: (row(bb, c), g))
    per_dir = lambda z, row: pl.BlockSpec((1, CHUNK, width), lambda bb, g, c: (z, row(bb, c), g))
    return pl.pallas_call(
        functools.partial(_scan_kernel, pairs=pairs),
        out_shape=(jax.ShapeDtypeStruct((1, n, d), F32), jax.ShapeDtypeStruct((1, n, d), F32)),
        grid=(batch, groups, nch),
        in_specs=[shared(fwd_row)] * 3 + [shared(rev_row)] * 3 + [per_dir(0, fwd_row)] * 3 + [per_dir(1, rev_row)] * 3,
        out_specs=(per_dir(0, fwd_row), per_dir(0, rev_row)),
        scratch_shapes=[pltpu.VMEM((2 * pairs, LANES, LANES), F32)],
        compiler_params=_cparams("arbitrary", "arbitrary", "arbitrary"),
        name="rwkv_scan",
    )(r, v, kk, r, v, kk, lw, kd, b, lw, kd, b)


def _rwkv_out_kernel(y0_ref, y1_ref, r_ref, kd0_ref, kd1_ref, v_ref, g_ref, x_ref, m_ref, rk_ref, lnw_ref,
                     lnb_ref, wo_ref, seg_ref, segt_ref, o_ref):
    seg, segt = seg_ref[...], segt_ref[...]
    y = y0_ref[0] + y1_ref[0]
    mean = _seg_sum(y, seg, segt) * (1.0 / HEAD_DIM)
    dy = y - mean
    var = _seg_sum(dy * dy, seg, segt) * (1.0 / HEAD_DIM)
    yn = (dy * lax.rsqrt(var + GN_EPS)) * lnw_ref[...] + lnb_ref[...]
    bonus = _seg_sum(r_ref[...] * (kd0_ref[0] + kd1_ref[0]) * rk_ref[...], seg, segt)
    out = (yn + bonus * v_ref[...]) * g_ref[...]
    o = _dot(out.astype(BF16), wo_ref[...])
    o_ref[...] = x_ref[...] + m_ref[0][2:3] * o


def _rwkv_readout(yf, yr, r, kd, v, g, x2, mod, p, seg, segt, tiles_per_sample):
    n, d = x2.shape
    row_spec = pl.BlockSpec((TM, d), lambda i: (i, 0))
    return pl.pallas_call(
        _rwkv_out_kernel,
        out_shape=jax.ShapeDtypeStruct((n, d), F32),
        grid=(n // TM,),
        in_specs=[
            pl.BlockSpec((1, TM, d), lambda i: (0, i, 0)),
            pl.BlockSpec((1, TM, d), lambda i: (0, i, 0)),
            row_spec,
            pl.BlockSpec((1, TM, d), lambda i: (0, i, 0)),
            pl.BlockSpec((1, TM, d), lambda i: (1, i, 0)),
            row_spec, row_spec, row_spec,
            pl.BlockSpec((1, N_MOD, d), lambda i: (_mod_row(i, tiles_per_sample), 0, 0)),
            _const_spec((1, d)), _const_spec((1, d)), _const_spec((1, d)),
            _const_spec((d, d)),
            _const_spec((d, LANES)), _const_spec((LANES, d)),
        ],
        out_specs=row_spec,
        compiler_params=_cparams("arbitrary"),
        name="rwkv_readout",
    )(yf, yr, r, kd, kd, v, g, x2, mod, p["r_k"], p["ln_w"], p["ln_b"], p["w_o"], seg, segt)


def _rope(x, cos, sin_lo, sin_hi):
    w = x.shape[1]
    reps = w // LANES
    tile = lambda t: jnp.concatenate([t] * reps, axis=1) if reps > 1 else t
    half = HEAD_DIM // 4
    return x * tile(cos) + pltpu.roll(x, w - half, 1) * tile(sin_lo) + pltpu.roll(x, half, 1) * tile(sin_hi)


def _gqa_proj_kernel(x_ref, m_ref, gain_ref, w_ref, qg_ref, kg_ref, cos_ref, slo_ref, shi_ref, seg_ref, segt_ref,
                     segk_ref, segtk_ref, q_o, k_o, v_o, *, d, dkv):
    m = m_ref[0]
    h = _norm_mod(x_ref[...], gain_ref[...], m[0:1], m[1:2]).astype(BF16)
    qkv = _dot(h, w_ref[...])
    q, k, v = qkv[:, :d], qkv[:, d:d + dkv], qkv[:, d + dkv:]
    cos, slo, shi = cos_ref[...], slo_ref[...], shi_ref[...]
    qms = _seg_sum(q * q, seg_ref[...], segt_ref[...]) * (1.0 / HEAD_DIM)
    q = (q * lax.rsqrt(qms + NORM_EPS)) * qg_ref[...]
    kms = _seg_sum(k * k, segk_ref[...], segtk_ref[...]) * (1.0 / HEAD_DIM)
    k = (k * lax.rsqrt(kms + NORM_EPS)) * kg_ref[...]
    q = (_rope(q, cos, slo, shi) * (HEAD_DIM ** -0.5 * LOG2_E)).astype(BF16)
    k = _rope(k, cos, slo, shi).astype(BF16)
    v = v.astype(BF16)
    lane = lax.broadcasted_iota(jnp.int32, (v.shape[0], LANES - HEAD_DIM), 1)
    ones_pad = jnp.where(lane == 0, 1.0, 0.0).astype(BF16)
    for hh in range(d // HEAD_DIM):
        q_o[hh] = q[:, hh * HEAD_DIM:(hh + 1) * HEAD_DIM]
    for hh in range(dkv // HEAD_DIM):
        k_o[hh] = k[:, hh * HEAD_DIM:(hh + 1) * HEAD_DIM]
        v_o[hh] = jnp.concatenate([v[:, hh * HEAD_DIM:(hh + 1) * HEAD_DIM], ones_pad], axis=1)


def _gqa_project(x2, mod, gain, p, rope, seg, segt, segk, segtk, tiles_per_sample):
    n, d = x2.shape
    dkv = ATTN_KV_HEADS * HEAD_DIM
    nq, nkv = d // HEAD_DIM, ATTN_KV_HEADS
    heads = lambda h: pl.BlockSpec((h, TM, HEAD_DIM), lambda i: (0, i, 0))
    pos = pl.BlockSpec((TM, LANES), lambda i: (i % tiles_per_sample, 0))
    return pl.pallas_call(
        functools.partial(_gqa_proj_kernel, d=d, dkv=dkv),
        out_shape=(jax.ShapeDtypeStruct((nq, n, HEAD_DIM), BF16), jax.ShapeDtypeStruct((nkv, n, HEAD_DIM), BF16),
                   jax.ShapeDtypeStruct((nkv, n, LANES), BF16)),
        grid=(n // TM,),
        in_specs=[
            pl.BlockSpec((TM, d), lambda i: (i, 0)),
            pl.BlockSpec((1, N_MOD, d), lambda i: (_mod_row(i, tiles_per_sample), 0, 0)),
            _const_spec((1, d)),
            _const_spec((d, d + 2 * dkv)),
            _const_spec((1, d)), _const_spec((1, dkv)),
            pos, pos, pos,
            _const_spec((d, LANES)), _const_spec((LANES, d)),
            _const_spec((dkv, LANES)), _const_spec((LANES, dkv)),
        ],
        out_specs=(heads(nq), heads(nkv), pl.BlockSpec((nkv, TM, LANES), lambda i: (0, i, 0))),
        compiler_params=_cparams("arbitrary"),
        name="gqa_project",
    )(x2, mod, gain, p["w_qkv"], p["q_gain"], p["k_gain"], rope[0], rope[1], rope[2], seg, segt, segk, segtk)


def _attn_kernel(q_ref, k_ref, v_ref, o_ref, *, group, kvs, ctx_rows):
    def attend(kv_rows):
        for kv in range(kvs):
            k = k_ref[kv, :kv_rows, :]
            v = v_ref[kv, :kv_rows, :]
            for hh in range(kv * group, (kv + 1) * group):
                s = _dot_nt(q_ref[hh], k)
                pr = jnp.exp2(s - jnp.max(s, axis=-1, keepdims=True))
                oe = _dot(pr.astype(BF16), v)
                o_ref[hh] = (oe[:, :HEAD_DIM] / oe[:, HEAD_DIM:HEAD_DIM + 1]).astype(BF16)

    is_ctx = pl.program_id(2) == 0
    pl.when(is_ctx)(lambda: attend(ctx_rows))
    pl.when(jnp.logical_not(is_ctx))(lambda: attend(k_ref.shape[1]))


def _attention(q, k, v, batch, tiles_per_sample, ctx_rows, kvs=2):
    nq, n, _ = q.shape
    nkv = k.shape[0]
    group = nq // nkv
    tt = n // batch
    q_spec = pl.BlockSpec((kvs * group, TM, HEAD_DIM), lambda bb, g, t: (g, bb * tiles_per_sample + t, 0))
    return pl.pallas_call(
        functools.partial(_attn_kernel, group=group, kvs=kvs, ctx_rows=ctx_rows),
        out_shape=jax.ShapeDtypeStruct((nq, n, HEAD_DIM), BF16),
        grid=(batch, nkv // kvs, tiles_per_sample),
        in_specs=[q_spec, pl.BlockSpec((kvs, tt, HEAD_DIM), lambda bb, g, t: (g, bb, 0)),
                  pl.BlockSpec((kvs, tt, LANES), lambda bb, g, t: (g, bb, 0))],
        out_specs=q_spec,
        compiler_params=_cparams("arbitrary", "arbitrary", "arbitrary"),
        name="gqa_attention",
    )(q, k, v)


def _attn_out_kernel(o_ref, x_ref, m_ref, wo_ref, y_ref):
    o = jnp.concatenate([o_ref[hh] for hh in range(o_ref.shape[0])], axis=1)
    y_ref[...] = x_ref[...] + m_ref[0][2:3] * _dot(o, wo_ref[...])


def _attn_out(o, x2, mod, w_o, tiles_per_sample):
    n, d = x2.shape
    nq = o.shape[0]
    row_spec = pl.BlockSpec((TM, d), lambda i: (i, 0))
    return pl.pallas_call(
        _attn_out_kernel,
        out_shape=jax.ShapeDtypeStruct((n, d), F32),
        grid=(n // TM,),
        in_specs=[
            pl.BlockSpec((nq, TM, HEAD_DIM), lambda i: (0, i, 0)),
            row_spec,
            pl.BlockSpec((1, N_MOD, d), lambda i: (_mod_row(i, tiles_per_sample), 0, 0)),
            _const_spec((d, d)),
        ],
        out_specs=row_spec,
        compiler_params=_cparams("arbitrary"),
        name="gqa_out_proj",
    )(o, x2, mod, w_o)


def _router_kernel(x_ref, m_ref, gain_ref, wr_ref, f_o, afft_o):
    m = m_ref[0]
    f = _norm_mod(x_ref[...], gain_ref[...], m[3:4], m[4:5])
    f_o[...] = f.reshape(f_o.shape)
    logits = _mm3(f, wr_ref[...])
    lane = lax.broadcasted_iota(jnp.int32, logits.shape, 1)
    logits = jnp.where(lane < N_EXPERTS, logits, -1e30)
    e = jnp.exp(logits - jnp.max(logits, axis=-1, keepdims=True))
    aff = e / jnp.sum(e, axis=-1, keepdims=True)
    afft_o[0] = aff.T[:N_EXPERTS]


def _moe_router(x2, mod, gain, w_router, batch, tiles_per_sample):
    n, d = x2.shape
    tt = n // batch
    return pl.pallas_call(
        _router_kernel,
        out_shape=(jax.ShapeDtypeStruct((n, d // LANES, LANES), F32),
                   jax.ShapeDtypeStruct((batch, N_EXPERTS, tt), F32)),
        grid=(n // TM,),
        in_specs=[
            pl.BlockSpec((TM, d), lambda i: (i, 0)),
            pl.BlockSpec((1, N_MOD, d), lambda i: (_mod_row(i, tiles_per_sample), 0, 0)),
            _const_spec((1, d)),
            _const_spec((d, LANES)),
        ],
        out_specs=(pl.BlockSpec((TM, d // LANES, LANES), lambda i: (i, 0, 0)),
                   pl.BlockSpec((1, N_EXPERTS, TM), lambda i: (i // tiles_per_sample, 0, i % tiles_per_sample))),
        compiler_params=_cparams("arbitrary"),
        name="moe_router",
    )(x2, mod, gain, w_router)


def _prefix_excl(mask, tri_excl):
    xb = mask.astype(BF16)
    carry = jnp.zeros((mask.shape[0], 1), F32)
    outs = []
    for blk in range(mask.shape[1] // LANES):
        piece = xb[:, blk * LANES:(blk + 1) * LANES]
        outs.append(_dot(piece, tri_excl) + carry)
        carry = carry + jnp.sum(piece.astype(F32), axis=1, keepdims=True)
    return jnp.concatenate(outs, axis=1) if len(outs) > 1 else outs[0]


def _top_cap(a, cap, tri_excl):
    ai = lax.bitcast_convert_type(a, jnp.int32)

    def body(i, thr):
        cand = thr | jnp.left_shift(jnp.int32(1), F32_VALUE_BITS - 1 - i)
        cnt = jnp.sum((ai >= cand).astype(F32), axis=1, keepdims=True)
        return jnp.where(cnt >= cap, cand, thr)

    thr = lax.fori_loop(0, F32_VALUE_BITS, body, jnp.zeros((a.shape[0], 1), jnp.int32))
    gt = ai > thr
    eq = ai == thr
    need = cap - jnp.sum(gt.astype(F32), axis=1, keepdims=True)
    sel = jnp.logical_or(gt, jnp.logical_and(eq, _prefix_excl(eq, tri_excl) < need))
    return sel, _prefix_excl(sel, tri_excl)


def _select_kernel(afft_ref, tri_ref, idx_o, gate_o, sel_s, pos_s, *, sets):
    e = pl.program_id(1)

    @pl.when(e == 0)
    def _():
        for off, n, cap, _ in sets:
            sel, pos = _top_cap(afft_ref[0, :, off:off + n], cap, tri_ref[...])
            sel_s[:, off:off + n] = sel.astype(F32)
            pos_s[:, off:off + n] = pos

    for off, n, cap, slot0 in sets:
        a_e = afft_ref[0, pl.ds(e, 1), off:off + n]
        sel_e = sel_s[pl.ds(e, 1), off:off + n]
        pos_e = pos_s[pl.ds(e, 1), off:off + n]
        wl = min(cap, LANES)
        nw = cap // wl
        pos_i = pos_e.astype(jnp.int32)
        lane_of = jnp.bitwise_and(pos_i, wl - 1)
        win_of = jnp.right_shift(pos_i, wl.bit_length() - 1)
        lane_id = lax.broadcasted_iota(jnp.int32, (wl, n), 0)
        onehot = jnp.where(jnp.logical_and(lane_of == lane_id, sel_e > 0), 1.0, 0.0).astype(BF16)
        tok = lax.broadcasted_iota(jnp.int32, (BF16_ROWS, n), 1)
        rid = lax.broadcasted_iota(jnp.int32, (BF16_ROWS, n), 0)
        a_h, a_m, a_l = _split3(a_e)
        rows = jnp.where(rid == 0, jnp.right_shift(tok, TOK_RADIX.bit_length() - 1).astype(F32),
               jnp.where(rid == 1, jnp.bitwise_and(tok, TOK_RADIX - 1).astype(F32),
               jnp.where(rid == 2, a_h.astype(F32),
               jnp.where(rid == 3, a_m.astype(F32),
               jnp.where(rid == 4, a_l.astype(F32), 0.0)))))
        stacked = jnp.concatenate([jnp.where(win_of == wi, rows, 0.0) for wi in range(nw)], axis=0).astype(BF16)
        res_all = _dot_nt(stacked, onehot)
        for wi in range(nw):
            res = res_all[BF16_ROWS * wi:BF16_ROWS * (wi + 1)]
            lo = slot0 + wi * wl
            idx_o[0, 0, :, lo:lo + wl] = (res[0:1] * TOK_RADIX + res[1:2]).astype(jnp.int32) + off
            gate_o[0, 0, :, lo:lo + wl] = res[2:3] + (res[3:4] + res[4:5])


def _moe_select(afft, sets, n_slots):
    batch, ne, tt = afft.shape
    tri = (jnp.arange(LANES)[:, None] < jnp.arange(LANES)[None, :]).astype(BF16)
    slot_spec = pl.BlockSpec((1, 1, 1, n_slots), lambda bb, e: (bb, e, 0, 0))
    return pl.pallas_call(
        functools.partial(_select_kernel, sets=sets),
        out_shape=(jax.ShapeDtypeStruct((batch, ne, 1, n_slots), jnp.int32),
                   jax.ShapeDtypeStruct((batch, ne, 1, n_slots), F32)),
        grid=(batch, ne),
        in_specs=[pl.BlockSpec((1, ne, tt), lambda bb, e: (bb, 0, 0)), _const_spec((LANES, LANES))],
        out_specs=(slot_spec, slot_spec),
        scratch_shapes=[pltpu.VMEM((ne, tt), F32), pltpu.VMEM((ne, tt), F32)],
        compiler_params=_cparams("arbitrary", "arbitrary"),
        name="moe_select",
    )(afft, tri)


def _gather_kernel(idx_ref, f_ref, xe_o, buf, *, n_slots):
    def body(c, _):
        buf[c] = f_ref[idx_ref[0, 0, 0, c]]
        return 0

    lax.fori_loop(0, n_slots, body, 0, unroll=8)
    xe_o[0, 0] = buf[...].reshape(xe_o.shape[2:]).astype(BF16)


def _moe_gather(idx, f, batch):
    n, sub, lanes = f.shape
    d = sub * lanes
    tt = n // batch
    ne, n_slots = idx.shape[1], idx.shape[3]
    return pl.pallas_call(
        functools.partial(_gather_kernel, n_slots=n_slots),
        out_shape=jax.ShapeDtypeStruct((batch, ne, n_slots, d), BF16),
        grid=(batch, ne),
        in_specs=[
            pl.BlockSpec((1, 1, 1, n_slots), lambda bb, e: (bb, e, 0, 0), memory_space=pltpu.SMEM),
            pl.BlockSpec((tt, sub, lanes), lambda bb, e: (bb, 0, 0)),
        ],
        out_specs=pl.BlockSpec((1, 1, n_slots, d), lambda bb, e: (bb, e, 0, 0)),
        scratch_shapes=[pltpu.VMEM((n_slots, sub, lanes), F32)],
        compiler_params=_cparams("arbitrary", "arbitrary"),
        name="moe_gather",
    )(idx, f)


def _expert_kernel(xe_ref, wg_ref, wu_ref, wd_ref, ml_ref, mc_ref, ye_o, *, cap_l):
    first = pl.program_id(2) == 0
    last = pl.program_id(2) == pl.num_programs(2) - 1
    wg = wg_ref[0, 0].astype(BF16)
    wu = wu_ref[0, 0].astype(BF16)
    wd = wd_ref[0, 0].astype(BF16)
    for i in range(xe_ref.shape[0]):
        x = xe_ref[i, 0]
        h = (_silu(_dot(x, wg)) * _dot(x, wu)).astype(BF16)
        y = _dot(h, wd)
        ye_o[i, 0] = jnp.where(first, y, ye_o[i, 0] + y)

    @pl.when(last)
    def _():
        latent_slot = lax.broadcasted_iota(jnp.int32, ye_o.shape[2:], 0) < cap_l
        for i in range(xe_ref.shape[0]):
            ye_o[i, 0] = ye_o[i, 0] * jnp.where(latent_slot, ml_ref[i], mc_ref[...])


def _moe_experts(xe, w_gate_up, w_down, gate_lat, gate_ctx, layer, cap_l, halves=2, tf=256):
    batch, ne, s, d = xe.shape
    de = w_down.shape[2]
    nf = de // tf
    bh = batch // halves
    tok = pl.BlockSpec((bh, 1, s, d), lambda e, mh, f: (mh, e, 0, 0))
    return pl.pallas_call(
        functools.partial(_expert_kernel, cap_l=cap_l),
        out_shape=jax.ShapeDtypeStruct((batch, ne, s, d), F32),
        grid=(ne, halves, nf),
        in_specs=[
            tok,
            pl.BlockSpec((1, 1, d, tf), lambda e, mh, f: (layer, e, 0, f)),
            pl.BlockSpec((1, 1, d, tf), lambda e, mh, f: (layer, e, 0, nf + f)),
            pl.BlockSpec((1, 1, tf, d), lambda e, mh, f: (layer, e, f, 0)),
            pl.BlockSpec((bh, 1, d), lambda e, mh, f: (mh, 0, 0)),
            _const_spec((1, d)),
        ],
        out_specs=tok,
        compiler_params=_cparams("arbitrary", "arbitrary", "arbitrary"),
        name="moe_experts",
    )(xe, w_gate_up, w_gate_up, w_down, gate_lat, gate_ctx)


def _combine_kernel(idx_ref, gate_ref, ye_ref, x_hbm, out_o, sem, *, n_slots):
    bb = pl.program_id(0)
    rows = out_o.shape[0]

    @pl.when(pl.program_id(1) == 0)
    def _():
        seed = pltpu.make_async_copy(x_hbm.at[pl.ds(bb * rows, rows)], out_o, sem)
        seed.start()
        seed.wait()

    def body(g, _):
        base = pl.multiple_of(g * SCATTER_GROUP, SCATTER_GROUP)
        rs = [idx_ref[0, 0, 0, base + j] for j in range(SCATTER_GROUP)]
        rows = ye_ref[0, 0, pl.ds(base, SCATTER_GROUP), :]
        new = [out_o[pl.ds(r, 1), :] + gate_ref[0, 0, 0, base + j] * rows[j:j + 1, :] for j, r in enumerate(rs)]
        for r, v in zip(rs, new):
            out_o[pl.ds(r, 1), :] = v
        return 0

    assert n_slots % SCATTER_GROUP == 0
    lax.fori_loop(0, n_slots // SCATTER_GROUP, body, 0)


def _moe_combine(idx, gate, ye, x2, tt):
    batch, ne, n_slots, d = ye.shape
    smem = pl.BlockSpec((1, 1, 1, n_slots), lambda bb, e: (bb, e, 0, 0), memory_space=pltpu.SMEM)
    return pl.pallas_call(
        functools.partial(_combine_kernel, n_slots=n_slots),
        out_shape=jax.ShapeDtypeStruct((batch * tt, d), F32),
        grid=(batch, ne),
        in_specs=[smem, smem, pl.BlockSpec((1, 1, n_slots, d), lambda bb, e: (bb, e, 0, 0)),
                  pl.BlockSpec(memory_space=pl.ANY)],
        out_specs=pl.BlockSpec((tt, d), lambda bb, e: (bb, 0)),
        scratch_shapes=[pltpu.SemaphoreType.DMA(())],
        compiler_params=_cparams("arbitrary", "arbitrary"),
        name="moe_combine",
    )(idx, gate, ye, x2)


def _final_kernel(x_ref, g_ref, o_ref):
    x = x_ref[...]
    ms = jnp.mean(x * x, axis=-1, keepdims=True)
    o_ref[0] = (x * lax.rsqrt(ms + NORM_EPS)) * g_ref[...]


def _final_norm(x2, gain, batch, tiles_per_sample):
    n, d = x2.shape
    lat_tiles = tiles_per_sample - 1
    return pl.pallas_call(
        _final_kernel,
        out_shape=jax.ShapeDtypeStruct((batch, lat_tiles * TM, d), F32),
        grid=(batch, lat_tiles),
        in_specs=[pl.BlockSpec((TM, d), lambda bb, t: (bb * tiles_per_sample + 1 + t, 0)), _const_spec((1, d))],
        out_specs=pl.BlockSpec((1, TM, d), lambda bb, t: (bb, t, 0)),
        compiler_params=_cparams("arbitrary", "arbitrary"),
        name="final_norm",
    )(x2, gain)


def _segment_matrices(width):
    heads = jnp.arange(width) // HEAD_DIM
    seg = (heads[:, None] == jnp.arange(LANES)[None, :]).astype(BF16)
    return seg, seg.T


def _rope_tables(seq, ctx_len):
    t = jnp.arange(seq)
    pos = jnp.stack([(t // GRID_W).astype(F32), (t % GRID_W).astype(F32)], axis=-1)
    half = HEAD_DIM // 4
    inv_freq = ROPE_THETA ** (-jnp.arange(0, 2 * half, 2, dtype=F32) / (2 * half))
    ang = pos[:, :, None] * inv_freq
    cos, sin = jnp.cos(ang), jnp.sin(ang)
    zero = jnp.zeros_like(sin)
    per_head = lambda first, second: jnp.concatenate([first, second], axis=-1).reshape(seq, HEAD_DIM)
    cos_h = per_head(cos, cos)
    sin_lo = per_head(-sin, zero)
    sin_hi = per_head(zero, sin)
    pad = lambda tab, fill: jnp.concatenate([jnp.full((ctx_len, HEAD_DIM), fill, F32), tab], axis=0)
    two = lambda tab: jnp.concatenate([tab, tab], axis=1)
    return two(pad(cos_h, 1.0)), two(pad(sin_lo, 0.0)), two(pad(sin_hi, 0.0))


def kernel(x, c, ctx, c_ctx, mod_w, mod_b, norm_mix, norm_ffn, rwkv_mu, rwkv_w_rkv, rwkv_w0, rwkv_w1, rwkv_w2,
           rwkv_a0, rwkv_a1, rwkv_a2, rwkv_g1, rwkv_g2, rwkv_k_k, rwkv_k_a, rwkv_r_k, rwkv_ln_w, rwkv_ln_b,
           rwkv_w_o, attn_w_qkv, attn_q_gain, attn_k_gain, attn_w_o, moe_router, moe_w_gate_up, moe_w_down,
           final_norm):
    batch, seq, d = x.shape
    ctx_len = ctx.shape[1]
    depth = mod_w.shape[0]
    tt = ctx_len + seq
    tps = tt // TM
    assert ctx_len == TM and seq % TM == 0 and CHUNK == HEAD_DIM and batch + 1 <= 16
    nq = d // HEAD_DIM

    x2 = jnp.concatenate([ctx, x], axis=1).reshape(batch * tt, d)
    cc = jnp.zeros((16, d), F32).at[0].set(c_ctx).at[1:batch + 1].set(c)
    mods = _mod_vectors(cc, mod_w, mod_b).reshape(depth, 16, N_MOD, d)
    seg, segt = _segment_matrices(d)
    segk, segtk = _segment_matrices(ATTN_KV_HEADS * HEAD_DIM)
    rope = _rope_tables(seq, ctx_len)
    cap_l = EC_CAPACITY * seq // N_EXPERTS
    cap_c = EC_CAPACITY * ctx_len // N_EXPERTS
    sets = ((ctx_len, seq, cap_l, 0), (0, ctx_len, cap_c, cap_l))
    router_pad = jnp.zeros((depth, d, LANES), F32).at[:, :, :N_EXPERTS].set(moe_router)

    ia = ib = 0
    for i in range(depth):
        mod = mods[i]
        if i % 2 == 0:
            zpad = jnp.zeros((HEAD_DIM, d), F32)
            lora_pad = lambda w: jnp.stack([jnp.concatenate([w[0], zpad], 0), jnp.concatenate([zpad, w[1]], 0)]).astype(BF16)
            p = {
                "mu": rwkv_mu[ia], "w_rkv": rwkv_w_rkv[ia].astype(BF16),
                "w1": jnp.concatenate([rwkv_w1[ia, 0], rwkv_w1[ia, 1]], axis=1).astype(BF16), "w2": lora_pad(rwkv_w2[ia]),
                "w0": rwkv_w0[ia],
                "a1": jnp.concatenate([rwkv_a1[ia, 0], rwkv_a1[ia, 1]], axis=1).astype(BF16), "a2": lora_pad(rwkv_a2[ia]),
                "a0": rwkv_a0[ia],
                "g1": rwkv_g1[ia].astype(BF16), "g2": rwkv_g2[ia].astype(BF16),
                "k_k": rwkv_k_k[ia].reshape(1, d), "k_a": rwkv_k_a[ia].reshape(1, d),
                "r_k": rwkv_r_k[ia].reshape(1, d), "ln_w": rwkv_ln_w[ia].reshape(1, d), "ln_b": rwkv_ln_b[ia].reshape(1, d),
                "w_o": rwkv_w_o[ia].astype(BF16),
            }
            r, v, kk, g, lw, kd, b = _rwkv_inputs(x2, mod, norm_mix[i].reshape(1, d), p, seg, segt, tps)
            yf, yr = _rwkv_scan(r, v, kk, lw, kd, b, batch, ctx_len)
            x2 = _rwkv_readout(yf, yr, r, kd, v, g, x2, mod, p, seg, segt, tps)
            ia += 1
        else:
            p = {
                "w_qkv": attn_w_qkv[ib].astype(BF16),
                "q_gain": jnp.tile(attn_q_gain[ib], nq).reshape(1, d),
                "k_gain": jnp.tile(attn_k_gain[ib], ATTN_KV_HEADS).reshape(1, ATTN_KV_HEADS * HEAD_DIM),
            }
            q, k, v = _gqa_project(x2, mod, norm_mix[i].reshape(1, d), p, rope, seg, segt, segk, segtk, tps)
            o = _attention(q, k, v, batch, tps, ctx_len)
            x2 = _attn_out(o, x2, mod, attn_w_o[ib].astype(BF16), tps)
            ib += 1
        f, afft = _moe_router(x2, mod, norm_ffn[i].reshape(1, d), router_pad[i], batch, tps)
        layer_sets = sets if i < depth - 1 else sets[:1]
        idx, gate = _moe_select(afft, layer_sets, sum(s[2] for s in layer_sets))
        xe = _moe_gather(idx, f, batch)
        ye = _moe_experts(xe, moe_w_gate_up, moe_w_down, mod[1:batch + 1, 5:6], mod[0, 5:6], i, cap_l)
        x2 = _moe_combine(idx, gate, ye, x2, tt)
    return _final_norm(x2, final_norm.reshape(1, d), batch, tps)
```

```python
import functools

import jax
import jax.numpy as jnp
from jax import lax
from jax.experimental import pallas as pl
from jax.experimental.pallas import tpu as pltpu

F32 = jnp.float32
BF16 = jnp.bfloat16

HEAD_DIM = 64
N_MOD = 6
NORM_EPS = 1e-6
GN_EPS = 64e-5
ROPE_THETA = 10000.0
GRID_W = 64
ATTN_KV_HEADS = 4
N_EXPERTS = 16
EC_CAPACITY = 2
TM = 256
CHUNK = 64
LANES = 128
VMEM_LIMIT = 56 * 1024 * 1024
NEG_EXP_M05 = -0.6065306597126334
LOG2_E = 1.4426950408889634
F32_VALUE_BITS = 31
BF16_ROWS = 16
TOK_RADIX = 64
SCATTER_GROUP = 8


def _cparams(*sem):
    return pltpu.CompilerParams(dimension_semantics=sem, vmem_limit_bytes=VMEM_LIMIT)


def _split2(x):
    hi = x.astype(BF16)
    lo = (x - hi.astype(F32)).astype(BF16)
    return hi, lo


def _split3(x):
    hi = x.astype(BF16)
    r1 = x - hi.astype(F32)
    mid = r1.astype(BF16)
    lo = (r1 - mid.astype(F32)).astype(BF16)
    return hi, mid, lo


def _dot(a, b):
    return jnp.dot(a, b, preferred_element_type=F32)


def _dot_nt(a, b):
    return lax.dot_general(a, b, (((1,), (1,)), ((), ())), preferred_element_type=F32)


def _mm(a, b, nt=False):
    return (_dot_nt if nt else _dot)(a.astype(BF16), b.astype(BF16))


def _mm3(a, b):
    ah, al = _split2(a)
    bh, bl = _split2(b)
    return _dot(ah, bh) + (_dot(ah, bl) + _dot(al, bh))


def _dot_exact_rhs(a, b_bf16):
    h, l = _split2(a)
    return _dot(h, b_bf16) + _dot(l, b_bf16)


def _seg_sum(x, seg, segt):
    s = _dot_exact_rhs(x, seg)
    return _dot_exact_rhs(s, segt)


def _norm_mod(x, gain, shift, scale):
    ms = jnp.mean(x * x, axis=-1, keepdims=True)
    y = x * lax.rsqrt(ms + NORM_EPS)
    return (y * gain) * (1.0 + scale) + shift


def _sigmoid(x):
    return 0.5 * jnp.tanh(0.5 * x) + 0.5


def _silu(x):
    return x * _sigmoid(x)


def _mod_row(i, tiles_per_sample):
    return jnp.where(i % tiles_per_sample == 0, 0, 1 + i // tiles_per_sample)


def _const_spec(shape):
    nd = len(shape)
    return pl.BlockSpec(shape, lambda *_: (0,) * nd)


def _mod_kernel(c_ref, w_ref, b_ref, o_ref):
    s = _silu(c_ref[...])
    o_ref[0] = _mm3(s, w_ref[0]) + b_ref[0]


def _mod_vectors(cc, mod_w, mod_b):
    depth, d, n = mod_w.shape
    tn = 1536
    return pl.pallas_call(
        _mod_kernel,
        out_shape=jax.ShapeDtypeStruct((depth, 16, n), F32),
        grid=(depth, n // tn),
        in_specs=[
            pl.BlockSpec((16, d), lambda l, j: (0, 0)),
            pl.BlockSpec((1, d, tn), lambda l, j: (l, 0, j)),
            pl.BlockSpec((1, 1, tn), lambda l, j: (l, 0, j)),
        ],
        out_specs=pl.BlockSpec((1, 16, tn), lambda l, j: (l, 0, j)),
        compiler_params=_cparams("arbitrary", "arbitrary"),
        name="mod_vectors",
    )(cc, mod_w, mod_b.reshape(depth, 1, n))


def _rwkv_in_kernel(x_ref, xp_ref, xn_ref, m_ref, gain_ref, mu_ref, wrkv_ref, w1_ref, w2_ref, w0_ref,
                    a1_ref, a2_ref, a0_ref, g1_ref, g2_ref, kk_ref_, ka_ref, seg_ref, segt_ref,
                    r_o, v_o, kk_o, g_o, lw_o, kd_o, b_o, *, tiles_per_sample):
    i = pl.program_id(0)
    j = i % tiles_per_sample
    m = m_ref[0]
    shift, scale = m[0:1], m[1:2]
    gain = gain_ref[...]
    h = _norm_mod(x_ref[...], gain, shift, scale)
    has_prev = (j >= 2).astype(F32)
    has_next = jnp.logical_and(j >= 1, j <= tiles_per_sample - 2).astype(F32)
    hp_row = _norm_mod(xp_ref[7:8, :], gain, shift, scale) * has_prev
    hn_row = _norm_mod(xn_ref[0:1, :], gain, shift, scale) * has_next
    row = lax.broadcasted_iota(jnp.int32, h.shape, 0)
    h_prev = jnp.where(row == 0, hp_row, pltpu.roll(h, 1, 0))
    h_next = jnp.where(row == TM - 1, hn_row, pltpu.roll(h, TM - 1, 0))
    xx = 0.5 * (h_prev + h_next) - h
    mu = mu_ref[...]

    def mix(n):
        return (h + xx * mu[n:n + 1]).astype(BF16)

    r = _dot(mix(0), wrkv_ref[0])
    k = _dot(mix(1), wrkv_ref[1])
    v = _dot(mix(2), wrkv_ref[2])
    tw = jnp.tanh(_dot(mix(3), w1_ref[...])).astype(BF16)
    ua = _dot(mix(4), a1_ref[...]).astype(BF16)
    g = _dot(_sigmoid(_dot(mix(5), g1_ref[...])).astype(BF16), g2_ref[...])
    kk = k * kk_ref_[...]
    n2 = _seg_sum(kk * kk, seg_ref[...], segt_ref[...])
    kk = kk / jnp.maximum(jnp.sqrt(n2), 1e-12)
    r_o[...] = r.astype(BF16)
    v_o[...] = v.astype(BF16)
    kk_o[...] = kk.astype(BF16)
    g_o[...] = g.astype(BF16)
    ka = ka_ref[...]
    for z in range(2):
        w_pre = w0_ref[z:z + 1, :] + _dot(tw, w2_ref[z])
        lw_o[z] = NEG_EXP_M05 * _sigmoid(w_pre)
        a =_sigmoid(a0_ref[z:z + 1, :] + _dot(ua, a2_ref[z]))
        kd_o[z] = (k * (1.0 + (a - 1.0) * ka)).astype(BF16)
        b_o[z] = (kk * a).astype(BF16)


def _rwkv_inputs(x2, mod, gain, p, seg, segt, tiles_per_sample):
    n, d = x2.shape
    nt = n // TM
    blk8 = TM // 8
    last8 = n // 8 - 1
    row_spec = pl.BlockSpec((TM, d), lambda i: (i, 0))
    dir_spec = pl.BlockSpec((2, TM, d), lambda i: (0, i, 0))
    tok = jax.ShapeDtypeStruct((n, d), BF16)
    tok2 = jax.ShapeDtypeStruct((2, n, d), BF16)
    return pl.pallas_call(
        functools.partial(_rwkv_in_kernel, tiles_per_sample=tiles_per_sample),
        out_shape=(tok, tok, tok, tok, jax.ShapeDtypeStruct((2, n, d), F32), tok2, tok2),
        grid=(nt,),
        in_specs=[
            row_spec,
            pl.BlockSpec((8, d), lambda i: (jnp.maximum(i * blk8 - 1, 0), 0)),
            pl.BlockSpec((8, d), lambda i: (jnp.minimum((i + 1) * blk8, last8), 0)),
            pl.BlockSpec((1, N_MOD, d), lambda i: (_mod_row(i, tiles_per_sample), 0, 0)),
            _const_spec((1, d)),
            _const_spec((6, d)),
            _const_spec((3, d, d)),
            _const_spec((d, LANES)),
            _const_spec((2, LANES, d)),
            _const_spec((2, d)),
            _const_spec((d, LANES)),
            _const_spec((2, LANES, d)),
            _const_spec((2, d)),
            _const_spec((d, LANES)),
            _const_spec((LANES, d)),
            _const_spec((1, d)),
            _const_spec((1, d)),
            _const_spec((d, LANES)),
            _const_spec((LANES, d)),
        ],
        out_specs=(row_spec, row_spec, row_spec, row_spec, dir_spec, dir_spec, dir_spec),
        compiler_params=_cparams("arbitrary"),
        name="rwkv_inputs",
    )(x2, x2, x2, mod, gain, p["mu"], p["w_rkv"], p["w1"], p["w2"], p["w0"], p["a1"], p["a2"], p["a0"],
      p["g1"], p["g2"], p["k_k"], p["k_a"], seg, segt)


def _scan_consts(reverse):
    sgn = -1 if reverse else 1
    n2 = 2 * CHUNK
    row = lax.broadcasted_iota(jnp.int32, (n2, n2), 0)
    col = lax.broadcasted_iota(jnp.int32, (n2, n2), 1)
    same = (row // CHUNK) == (col // CHUNK)
    dt = (row % CHUNK - col % CHUNK) * sgn
    rc = lax.broadcasted_iota(jnp.int32, (CHUNK, CHUNK), 0)
    cc = lax.broadcasted_iota(jnp.int32, (CHUNK, CHUNK), 1)
    return {
        "same": same,
        "strict": jnp.logical_and(same, dt > 0),
        "incl": jnp.logical_and(same, dt >= 0),
        "eye": (row == col).astype(F32),
        "tri": ((rc - cc) * sgn >= 0).astype(BF16),
        "head0": lax.broadcasted_iota(jnp.int32, (CHUNK, LANES), 1) < HEAD_DIM,
    }


def _scan_units(units):
    n2 = 2 * CHUNK
    every = lambda f: [f(u) for u in units]

    def stack(u, x):
        return jnp.concatenate([jnp.where(u["c"]["head0"], x, 0.0), jnp.where(u["c"]["head0"], 0.0, x)], axis=0)

    def prep(u):
        cl = _dot_exact_rhs_left(u["c"]["tri"], u["lw"])
        tot = jnp.sum(u["lw"], axis=0, keepdims=True)
        e_ncl = jnp.exp(-cl)
        e_end = jnp.exp(tot - cl)
        u["q2"] = jnp.concatenate([stack(u, u["kk"] * jnp.exp(cl - u["lw"])), stack(u, u["r"] * jnp.exp(cl))], axis=0)
        u["k2"] = jnp.concatenate([stack(u, u["kd"] * e_ncl), stack(u, u["b"] * e_ncl)], axis=0)
        u["ket"] = jnp.concatenate([u["kd"] * e_end, -(u["b"] * e_end)], axis=0).T
        u["g_col"] = jnp.sum(jnp.where(u["c"]["eye"] > 0, jnp.exp(tot), 0.0), axis=1, keepdims=True)
        u["vs"] = stack(u, u["v"])

    every(prep)
    a_all = every(lambda u: _mm(u["q2"], u["k2"], nt=True))
    qm = every(lambda u: _mm(u["q2"], u["m0"]))
    for u, a in zip(units, a_all):
        c = u["c"]
        u["l_kk"] = jnp.where(c["strict"], a[:n2, :n2], 0.0)
        u["l_rk"] = jnp.where(c["incl"], a[n2:, :n2], 0.0)
        u["l_rb"] = jnp.where(c["incl"], a[n2:, n2:], 0.0)
        u["pw"] = -jnp.where(c["strict"], a[:n2, n2:], 0.0)
        u["inv"] = c["eye"] + u["pw"]
    lv = every(lambda u: _mm(u["l_kk"], u["vs"]))
    levels = CHUNK.bit_length() - 1
    sq = every(lambda u: _mm(u["pw"], u["pw"]))
    for u, x in zip(units, sq):
        u["pw"] = x
    for _ in range(1, levels - 1):
        st = every(lambda u: _mm(jnp.concatenate([u["pw"], u["inv"]], axis=0), u["pw"]))
        for u, x in zip(units, st):
            u["pw"] = x[:n2]
            u["inv"] = u["inv"] + x[n2:]
    last = every(lambda u: _mm(u["inv"], u["pw"]))
    for u, x in zip(units, last):
        u["inv"] = u["inv"] + x
    us = [_mm(u["inv"], q[:n2] + t) for u, q, t in zip(units, qm, lv)]
    ys = [q[n2:] + _mm(jnp.concatenate([u["l_rk"], -u["l_rb"]], axis=1), jnp.concatenate([u["vs"], s], axis=0))
          for u, q, s in zip(units, qm, us)]
    new = [_mm(u["ket"], jnp.concatenate([u["v"], s[:CHUNK] + s[CHUNK:]], axis=0)) for u, s in zip(units, us)]
    ms = [u["m0"] * u["g_col"] + jnp.where(u["c"]["same"], x, 0.0) for u, x in zip(units, new)]
    return [y[:CHUNK] + y[CHUNK:] for y in ys], ms


def _dot_exact_rhs_left(a_bf16, b):
    h, m, l = _split3(b)
    return _dot(a_bf16, h) + (_dot(a_bf16, m) + _dot(a_bf16, l))


def _scan_kernel(rf_ref, vf_ref, kkf_ref, rr_ref, vr_ref, kkr_ref, lwf_ref, kdf_ref, bf_ref, lwr_ref, kdr_ref,
                 br_ref, yf_ref, yr_ref, m_ref, *, pairs):
    @pl.when(pl.program_id(2) == 0)
    def _():
        m_ref[...] = jnp.zeros_like(m_ref)

    dirs = (
        (_scan_consts(False), rf_ref, vf_ref, kkf_ref, lwf_ref, kdf_ref, bf_ref, yf_ref),
        (_scan_consts(True), rr_ref, vr_ref, kkr_ref, lwr_ref, kdr_ref, br_ref, yr_ref),
    )
    units = []
    for z, (consts, r_ref, v_ref, kk_ref, lw_ref, kd_ref, b_ref, _) in enumerate(dirs):
        for p in range(pairs):
            sl = slice(p * LANES, (p + 1) * LANES)
            units.append({"c": consts, "r": r_ref[:, sl].astype(F32), "v": v_ref[:, sl].astype(F32),
                          "kk": kk_ref[:, sl].astype(F32), "lw": lw_ref[0, :, sl],
                          "kd": kd_ref[0, :, sl].astype(F32), "b": b_ref[0, :, sl].astype(F32),
                          "m0": m_ref[z * pairs + p]})
    ys, ms = _scan_units(units)
    for i, (y, m1) in enumerate(zip(ys, ms)):
        z, p = divmod(i, pairs)
        dirs[z][-1][0, :, p * LANES:(p + 1) * LANES] = y
        m_ref[i] = m1


def _rwkv_scan(r, v, kk, lw, kd, b, batch, ctx_len, pairs=8):
    n, d = r.shape
    tt = n // batch
    nch = tt // CHUNK
    nch_ctx = ctx_len // CHUNK
    width = pairs * LANES
    groups = d // width

    fwd_row = lambda bb, c: bb * nch + c
    rev_row = lambda bb, c: bb * nch + jnp.where(c < nch_ctx, nch_ctx - 1 - c, nch + nch_ctx - 1 - c)
    shared = lambda row: pl.BlockSpec((CHUNK, width), lambda bb, g, c: (row(bb, c), g))
    per_dir = lambda z, row: pl.BlockSpec((1, CHUNK, width), lambda bb, g, c: (z, row(bb, c), g))
    return pl.pallas_call(
        functools.partial(_scan_kernel, pairs=pairs),
        out_shape=(jax.ShapeDtypeStruct((1, n, d), F32), jax.ShapeDtypeStruct((1, n, d), F32)),
        grid=(batch, groups, nch),
        in_specs=[shared(fwd_row)] * 3 + [shared(rev_row)] * 3 + [per_dir(0, fwd_row)] * 3 + [per_dir(1, rev_row)] * 3,
        out_specs=(per_dir(0, fwd_row), per_dir(0, rev_row)),
        scratch_shapes=[pltpu.VMEM((2 * pairs, LANES, LANES), F32)],
        compiler_params=_cparams("arbitrary", "arbitrary", "arbitrary"),
        name="rwkv_scan",
    )(r, v, kk, r, v, kk, lw, kd, b, lw, kd, b)


def _rwkv_out_kernel(y0_ref, y1_ref, r_ref, kd0_ref, kd1_ref, v_ref, g_ref, x_ref, m_ref, rk_ref, lnw_ref,
                     lnb_ref, wo_ref, seg_ref, segt_ref, o_ref):
    seg, segt = seg_ref[...], segt_ref[...]
    y = y0_ref[0] + y1_ref[0]
    mean = _seg_sum(y, seg, segt) * (1.0 / HEAD_DIM)
    dy = y - mean
    var = _seg_sum(dy * dy, seg, segt) * (1.0 / HEAD_DIM)
    yn = (dy * lax.rsqrt(var + GN_EPS)) * lnw_ref[...] + lnb_ref[...]
    kd_sum = kd0_ref[0].astype(F32) + kd1_ref[0].astype(F32)
    bonus = _seg_sum(r_ref[...].astype(F32) * kd_sum * rk_ref[...], seg, segt)
    out = (yn + bonus * v_ref[...].astype(F32)) * g_ref[...].astype(F32)
    o = _dot(out.astype(BF16), wo_ref[...])
    o_ref[...] = x_ref[...] + m_ref[0][2:3] * o


def _rwkv_readout(yf, yr, r, kd, v, g, x2, mod, p, seg, segt, tiles_per_sample):
    n, d = x2.shape
    row_spec = pl.BlockSpec((TM, d), lambda i: (i, 0))
    return pl.pallas_call(
        _rwkv_out_kernel,
        out_shape=jax.ShapeDtypeStruct((n, d), F32),
        grid=(n // TM,),
        in_specs=[
            pl.BlockSpec((1, TM, d), lambda i: (0, i, 0)),
            pl.BlockSpec((1, TM, d), lambda i: (0, i, 0)),
            row_spec,
            pl.BlockSpec((1, TM, d), lambda i: (0, i, 0)),
            pl.BlockSpec((1, TM, d), lambda i: (1, i, 0)),
            row_spec, row_spec, row_spec,
            pl.BlockSpec((1, N_MOD, d), lambda i: (_mod_row(i, tiles_per_sample), 0, 0)),
            _const_spec((1, d)), _const_spec((1, d)), _const_spec((1, d)),
            _const_spec((d, d)),
            _const_spec((d, LANES)), _const_spec((LANES, d)),
        ],
        out_specs=row_spec,
        compiler_params=_cparams("arbitrary"),
        name="rwkv_readout",
    )(yf, yr, r, kd, kd, v, g, x2, mod, p["r_k"], p["ln_w"], p["ln_b"], p["w_o"], seg, segt)


def _rope(x, cos, sin_lo, sin_hi):
    w = x.shape[1]
    reps = w // LANES
    tile = lambda t: jnp.concatenate([t] * reps, axis=1) if reps > 1 else t
    half = HEAD_DIM // 4
    return x * tile(cos) + pltpu.roll(x, w - half, 1) * tile(sin_lo) + pltpu.roll(x, half, 1) * tile(sin_hi)


def _gqa_proj_kernel(x_ref, m_ref, gain_ref, w_ref, qg_ref, kg_ref, cos_ref, slo_ref, shi_ref, seg_ref, segt_ref,
                     segk_ref, segtk_ref, q_o, k_o, v_o, *, d, dkv):
    m = m_ref[0]
    h = _norm_mod(x_ref[...], gain_ref[...], m[0:1], m[1:2]).astype(BF16)
    qkv = _dot(h, w_ref[...])
    q, k, v = qkv[:, :d], qkv[:, d:d + dkv], qkv[:, d + dkv:]
    cos, slo, shi = cos_ref[...], slo_ref[...], shi_ref[...]
    qms = _seg_sum(q * q, seg_ref[...], segt_ref[...]) * (1.0 / HEAD_DIM)
    q = (q * lax.rsqrt(qms + NORM_EPS)) * qg_ref[...]
    kms = _seg_sum(k * k, segk_ref[...], segtk_ref[...]) * (1.0 / HEAD_DIM)
    k = (k * lax.rsqrt(kms + NORM_EPS)) * kg_ref[...]
    q = (_rope(q, cos, slo, shi) * (HEAD_DIM ** -0.5 * LOG2_E)).astype(BF16)
    k = _rope(k, cos, slo, shi).astype(BF16)
    v = v.astype(BF16)
    lane = lax.broadcasted_iota(jnp.int32, (v.shape[0], LANES - HEAD_DIM), 1)
    ones_pad = jnp.where(lane == 0, 1.0, 0.0).astype(BF16)
    for hh in range(d // HEAD_DIM):
        q_o[hh] = q[:, hh * HEAD_DIM:(hh + 1) * HEAD_DIM]
    for hh in range(dkv // HEAD_DIM):
        k_o[hh] = k[:, hh * HEAD_DIM:(hh + 1) * HEAD_DIM]
        v_o[hh] = jnp.concatenate([v[:, hh * HEAD_DIM:(hh + 1) * HEAD_DIM], ones_pad], axis=1)


def _gqa_project(x2, mod, gain, p, rope, seg, segt, segk, segtk, tiles_per_sample):
    n, d = x2.shape
    dkv = ATTN_KV_HEADS * HEAD_DIM
    nq, nkv = d // HEAD_DIM, ATTN_KV_HEADS
    heads = lambda h: pl.BlockSpec((h, TM, HEAD_DIM), lambda i: (0, i, 0))
    pos = pl.BlockSpec((TM, LANES), lambda i: (i % tiles_per_sample, 0))
    return pl.pallas_call(
        functools.partial(_gqa_proj_kernel, d=d, dkv=dkv),
        out_shape=(jax.ShapeDtypeStruct((nq, n, HEAD_DIM), BF16), jax.ShapeDtypeStruct((nkv, n, HEAD_DIM), BF16),
                   jax.ShapeDtypeStruct((nkv, n, LANES), BF16)),
        grid=(n // TM,),
        in_specs=[
            pl.BlockSpec((TM, d), lambda i: (i, 0)),
            pl.BlockSpec((1, N_MOD, d), lambda i: (_mod_row(i, tiles_per_sample), 0, 0)),
            _const_spec((1, d)),
            _const_spec((d, d + 2 * dkv)),
            _const_spec((1, d)), _const_spec((1, dkv)),
            pos, pos, pos,
            _const_spec((d, LANES)), _const_spec((LANES, d)),
            _const_spec((dkv, LANES)), _const_spec((LANES, dkv)),
        ],
        out_specs=(heads(nq), heads(nkv), pl.BlockSpec((nkv, TM, LANES), lambda i: (0, i, 0))),
        compiler_params=_cparams("arbitrary"),
        name="gqa_project",
    )(x2, mod, gain, p["w_qkv"], p["q_gain"], p["k_gain"], rope[0], rope[1], rope[2], seg, segt, segk, segtk)


def _attn_kernel(q_ref, k_ref, v_ref, o_ref, *, group, kvs, ctx_rows):
    def attend(kv_rows):
        for kv in range(kvs):
            k = k_ref[kv, :kv_rows, :]
            v = v_ref[kv, :kv_rows, :]
            for hh in range(kv * group, (kv + 1) * group):
                s = _dot_nt(q_ref[hh], k)
                pr = jnp.exp2(s - jnp.max(s, axis=-1, keepdims=True))
                oe = _dot(pr.astype(BF16), v)
                o_ref[hh] = (oe[:, :HEAD_DIM] / oe[:, HEAD_DIM:HEAD_DIM + 1]).astype(BF16)

    is_ctx = pl.program_id(2) == 0
    pl.when(is_ctx)(lambda: attend(ctx_rows))
    pl.when(jnp.logical_not(is_ctx))(lambda: attend(k_ref.shape[1]))


def _attention(q, k, v, batch, tiles_per_sample, ctx_rows, kvs=2):
    nq, n, _ = q.shape
    nkv = k.shape[0]
    group = nq // nkv
    tt = n // batch
    q_spec = pl.BlockSpec((kvs * group, TM, HEAD_DIM), lambda bb, g, t: (g, bb * tiles_per_sample + t, 0))
    return pl.pallas_call(
        functools.partial(_attn_kernel, group=group, kvs=kvs, ctx_rows=ctx_rows),
        out_shape=jax.ShapeDtypeStruct((nq, n, HEAD_DIM), BF16),
        grid=(batch, nkv // kvs, tiles_per_sample),
        in_specs=[q_spec, pl.BlockSpec((kvs, tt, HEAD_DIM), lambda bb, g, t: (g, bb, 0)),
                  pl.BlockSpec((kvs, tt, LANES), lambda bb, g, t: (g, bb, 0))],
        out_specs=q_spec,
        compiler_params=_cparams("arbitrary", "arbitrary", "arbitrary"),
        name="gqa_attention",
    )(q, k, v)


def _attn_out_kernel(o_ref, x_ref, m_ref, wo_ref, y_ref):
    o = jnp.concatenate([o_ref[hh] for hh in range(o_ref.shape[0])], axis=1)
    y_ref[...] = x_ref[...] + m_ref[0][2:3] * _dot(o, wo_ref[...])


def _attn_out(o, x2, mod, w_o, tiles_per_sample):
    n, d = x2.shape
    nq = o.shape[0]
    row_spec = pl.BlockSpec((TM, d), lambda i: (i, 0))
    return pl.pallas_call(
        _attn_out_kernel,
        out_shape=jax.ShapeDtypeStruct((n, d), F32),
        grid=(n // TM,),
        in_specs=[
            pl.BlockSpec((nq, TM, HEAD_DIM), lambda i: (0, i, 0)),
            row_spec,
            pl.BlockSpec((1, N_MOD, d), lambda i: (_mod_row(i, tiles_per_sample), 0, 0)),
            _const_spec((d, d)),
        ],
        out_specs=row_spec,
        compiler_params=_cparams("arbitrary"),
        name="gqa_out_proj",
    )(o, x2, mod, w_o)


def _router_kernel(x_ref, m_ref, gain_ref, wr_ref, f_o, afft_o):
    m = m_ref[0]
    f = _norm_mod(x_ref[...], gain_ref[...], m[3:4], m[4:5])
    f_o[...] = f.reshape(f_o.shape)
    logits = _mm3(f, wr_ref[...])
    lane = lax.broadcasted_iota(jnp.int32, logits.shape, 1)
    logits = jnp.where(lane < N_EXPERTS, logits, -1e30)
    e = jnp.exp(logits - jnp.max(logits, axis=-1, keepdims=True))
    aff = e / jnp.sum(e, axis=-1, keepdims=True)
    afft_o[0] = aff.T[:N_EXPERTS]


def _moe_router(x2, mod, gain, w_router, batch, tiles_per_sample):
    n, d = x2.shape
    tt = n // batch
    return pl.pallas_call(
        _router_kernel,
        out_shape=(jax.ShapeDtypeStruct((n, d // LANES, LANES), F32),
                   jax.ShapeDtypeStruct((batch, N_EXPERTS, tt), F32)),
        grid=(n // TM,),
        in_specs=[
            pl.BlockSpec((TM, d), lambda i: (i, 0)),
            pl.BlockSpec((1, N_MOD, d), lambda i: (_mod_row(i, tiles_per_sample), 0, 0)),
            _const_spec((1, d)),
            _const_spec((d, LANES)),
        ],
        out_specs=(pl.BlockSpec((TM, d // LANES, LANES), lambda i: (i, 0, 0)),
                   pl.BlockSpec((1, N_EXPERTS, TM), lambda i: (i // tiles_per_sample, 0, i % tiles_per_sample))),
        compiler_params=_cparams("arbitrary"),
        name="moe_router",
    )(x2, mod, gain, w_router)


def _prefix_excl(mask, tri_excl):
    xb = mask.astype(BF16)
    carry = jnp.zeros((mask.shape[0], 1), F32)
    outs = []
    for blk in range(mask.shape[1] // LANES):
        piece = xb[:, blk * LANES:(blk + 1) * LANES]
        outs.append(_dot(piece, tri_excl) + carry)
        carry = carry + jnp.sum(piece.astype(F32), axis=1, keepdims=True)
    return jnp.concatenate(outs, axis=1) if len(outs) > 1 else outs[0]


def _top_cap(a, cap, tri_excl):
    ai = lax.bitcast_convert_type(a, jnp.int32)

    def body(i, thr):
        cand = thr | jnp.left_shift(jnp.int32(1), F32_VALUE_BITS - 1 - i)
        cnt = jnp.sum((ai >= cand).astype(F32), axis=1, keepdims=True)
        return jnp.where(cnt >= cap, cand, thr)

    thr = lax.fori_loop(0, F32_VALUE_BITS, body, jnp.zeros((a.shape[0], 1), jnp.int32))
    gt = ai > thr
    eq = ai == thr
    need = cap - jnp.sum(gt.astype(F32), axis=1, keepdims=True)
    sel = jnp.logical_or(gt, jnp.logical_and(eq, _prefix_excl(eq, tri_excl) < need))
    return sel, _prefix_excl(sel, tri_excl)


def _select_kernel(afft_ref, tri_ref, idx_o, gate_o, sel_s, pos_s, *, sets):
    e = pl.program_id(1)

    @pl.when(e == 0)
    def _():
        for off, n, cap, _ in sets:
            sel, pos = _top_cap(afft_ref[0, :, off:off + n], cap, tri_ref[...])
            sel_s[:, off:off + n] = sel.astype(F32)
            pos_s[:, off:off + n] = pos

    for off, n, cap, slot0 in sets:
        a_e = afft_ref[0, pl.ds(e, 1), off:off + n]
        sel_e = sel_s[pl.ds(e, 1), off:off + n]
        pos_e = pos_s[pl.ds(e, 1), off:off + n]
        wl = min(cap, LANES)
        nw = cap // wl
        pos_i = pos_e.astype(jnp.int32)
        lane_of = jnp.bitwise_and(pos_i, wl - 1)
        win_of = jnp.right_shift(pos_i, wl.bit_length() - 1)
        lane_id = lax.broadcasted_iota(jnp.int32, (wl, n), 0)
        onehot = jnp.where(jnp.logical_and(lane_of == lane_id, sel_e > 0), 1.0, 0.0).astype(BF16)
        tok = lax.broadcasted_iota(jnp.int32, (BF16_ROWS, n), 1)
        rid = lax.broadcasted_iota(jnp.int32, (BF16_ROWS, n), 0)
        a_h, a_m, a_l = _split3(a_e)
        rows = jnp.where(rid == 0, jnp.right_shift(tok, TOK_RADIX.bit_length() - 1).astype(F32),
               jnp.where(rid == 1, jnp.bitwise_and(tok, TOK_RADIX - 1).astype(F32),
               jnp.where(rid == 2, a_h.astype(F32),
               jnp.where(rid == 3, a_m.astype(F32),
               jnp.where(rid == 4, a_l.astype(F32), 0.0)))))
        stacked = jnp.concatenate([jnp.where(win_of == wi, rows, 0.0) for wi in range(nw)], axis=0).astype(BF16)
        res_all = _dot_nt(stacked, onehot)
        for wi in range(nw):
            res = res_all[BF16_ROWS * wi:BF16_ROWS * (wi + 1)]
            lo = slot0 + wi * wl
            idx_o[0, 0, :, lo:lo + wl] = (res[0:1] * TOK_RADIX + res[1:2]).astype(jnp.int32) + off
            gate_o[0, 0, :, lo:lo + wl] = res[2:3] + (res[3:4] + res[4:5])


def _moe_select(afft, sets, n_slots):
    batch, ne, tt = afft.shape
    tri = (jnp.arange(LANES)[:, None] < jnp.arange(LANES)[None, :]).astype(BF16)
    slot_spec = pl.BlockSpec((1, 1, 1, n_slots), lambda bb, e: (bb, e, 0, 0))
    return pl.pallas_call(
        functools.partial(_select_kernel, sets=sets),
        out_shape=(jax.ShapeDtypeStruct((batch, ne, 1, n_slots), jnp.int32),
                   jax.ShapeDtypeStruct((batch, ne, 1, n_slots), F32)),
        grid=(batch, ne),
        in_specs=[pl.BlockSpec((1, ne, tt), lambda bb, e: (bb, 0, 0)), _const_spec((LANES, LANES))],
        out_specs=(slot_spec, slot_spec),
        scratch_shapes=[pltpu.VMEM((ne, tt), F32), pltpu.VMEM((ne, tt), F32)],
        compiler_params=_cparams("arbitrary", "arbitrary"),
        name="moe_select",
    )(afft, tri)


def _gather_kernel(idx_ref, f_ref, xe_o, buf, *, n_slots):
    def body(c, _):
        buf[c] = f_ref[idx_ref[0, 0, 0, c]]
        return 0

    lax.fori_loop(0, n_slots, body, 0, unroll=8)
    xe_o[0, 0] = buf[...].reshape(xe_o.shape[2:]).astype(BF16)


def _moe_gather(idx, f, batch):
    n, sub, lanes = f.shape
    d = sub * lanes
    tt = n // batch
    ne, n_slots = idx.shape[1], idx.shape[3]
    return pl.pallas_call(
        functools.partial(_gather_kernel, n_slots=n_slots),
        out_shape=jax.ShapeDtypeStruct((batch, ne, n_slots, d), BF16),
        grid=(batch, ne),
        in_specs=[
            pl.BlockSpec((1, 1, 1, n_slots), lambda bb, e: (bb, e, 0, 0), memory_space=pltpu.SMEM),
            pl.BlockSpec((tt, sub, lanes), lambda bb, e: (bb, 0, 0)),
        ],
        out_specs=pl.BlockSpec((1, 1, n_slots, d), lambda bb, e: (bb, e, 0, 0)),
        scratch_shapes=[pltpu.VMEM((n_slots, sub, lanes), F32)],
        compiler_params=_cparams("arbitrary", "arbitrary"),
        name="moe_gather",
    )(idx, f)


def _expert_kernel(xe_ref, wg_ref, wu_ref, wd_ref, ml_ref, mc_ref, ye_o, *, cap_l):
    first = pl.program_id(2) == 0
    last = pl.program_id(2) == pl.num_programs(2) - 1
    wg = wg_ref[0, 0].astype(BF16)
    wu = wu_ref[0, 0].astype(BF16)
    wd = wd_ref[0, 0].astype(BF16)
    for i in range(xe_ref.shape[0]):
        x = xe_ref[i, 0]
        h = (_silu(_dot(x, wg)) * _dot(x, wu)).astype(BF16)
        y = _dot(h, wd)
        ye_o[i, 0] = jnp.where(first, y, ye_o[i, 0] + y)

    @pl.when(last)
    def _():
        latent_slot = lax.broadcasted_iota(jnp.int32, ye_o.shape[2:], 0) < cap_l
        for i in range(xe_ref.shape[0]):
            ye_o[i, 0] = ye_o[i, 0] * jnp.where(latent_slot, ml_ref[i], mc_ref[...])


def _moe_experts(xe, w_gate_up, w_down, gate_lat, gate_ctx, layer, cap_l, halves=2, tf=256):
    batch, ne, s, d = xe.shape
    de = w_down.shape[2]
    nf = de // tf
    bh = batch // halves
    tok = pl.BlockSpec((bh, 1, s, d), lambda e, mh, f: (mh, e, 0, 0))
    return pl.pallas_call(
        functools.partial(_expert_kernel, cap_l=cap_l),
        out_shape=jax.ShapeDtypeStruct((batch, ne, s, d), F32),
        grid=(ne, halves, nf),
        in_specs=[
            tok,
            pl.BlockSpec((1, 1, d, tf), lambda e, mh, f: (layer, e, 0, f)),
            pl.BlockSpec((1, 1, d, tf), lambda e, mh, f: (layer, e, 0, nf + f)),
            pl.BlockSpec((1, 1, tf, d), lambda e, mh, f: (layer, e, f, 0)),
            pl.BlockSpec((bh, 1, d), lambda e, mh, f: (mh, 0, 0)),
            _const_spec((1, d)),
        ],
        out_specs=tok,
        compiler_params=_cparams("arbitrary", "arbitrary", "arbitrary"),
        name="moe_experts",
    )(xe, w_gate_up, w_gate_up, w_down, gate_lat, gate_ctx)


def _combine_kernel(idx_ref, gate_ref, ye_ref, x_hbm, out_o, sem, *, n_slots):
    bb = pl.program_id(0)
    rows = out_o.shape[0]

    @pl.when(pl.program_id(1) == 0)
    def _():
        seed = pltpu.make_async_copy(x_hbm.at[pl.ds(bb * rows, rows)], out_o, sem)
        seed.start()
        seed.wait()

    def body(g, _):
        base = pl.multiple_of(g * SCATTER_GROUP, SCATTER_GROUP)
        rs = [idx_ref[0, 0, 0, base + j] for j in range(SCATTER_GROUP)]
        rows = ye_ref[0, 0, pl.ds(base, SCATTER_GROUP), :]
        new = [out_o[pl.ds(r, 1), :] + gate_ref[0, 0, 0, base + j] * rows[j:j + 1, :] for j, r in enumerate(rs)]
        for r, v in zip(rs, new):
            out_o[pl.ds(r, 1), :] = v
        return 0

    assert n_slots % SCATTER_GROUP == 0
    lax.fori_loop(0, n_slots // SCATTER_GROUP, body, 0)


def _moe_combine(idx, gate, ye, x2, tt):
    batch, ne, n_slots, d = ye.shape
    smem = pl.BlockSpec((1, 1, 1, n_slots), lambda bb, e: (bb, e, 0, 0), memory_space=pltpu.SMEM)
    return pl.pallas_call(
        functools.partial(_combine_kernel, n_slots=n_slots),
        out_shape=jax.ShapeDtypeStruct((batch * tt, d), F32),
        grid=(batch, ne),
        in_specs=[smem, smem, pl.BlockSpec((1, 1, n_slots, d), lambda bb, e: (bb, e, 0, 0)),
                  pl.BlockSpec(memory_space=pl.ANY)],
        out_specs=pl.BlockSpec((tt, d), lambda bb, e: (bb, 0)),
        scratch_shapes=[pltpu.SemaphoreType.DMA(())],
        compiler_params=_cparams("arbitrary", "arbitrary"),
        name="moe_combine",
    )(idx, gate, ye, x2)


def _final_kernel(x_ref, g_ref, o_ref):
    x = x_ref[...]
    ms = jnp.mean(x * x, axis=-1, keepdims=True)
    o_ref[0] = (x * lax.rsqrt(ms + NORM_EPS)) * g_ref[...]


def _final_norm(x2, gain, batch, tiles_per_sample):
    n, d = x2.shape
    lat_tiles = tiles_per_sample - 1
    return pl.pallas_call(
        _final_kernel,
        out_shape=jax.ShapeDtypeStruct((batch, lat_tiles * TM, d), F32),
        grid=(batch, lat_tiles),
        in_specs=[pl.BlockSpec((TM, d), lambda bb, t: (bb * tiles_per_sample + 1 + t, 0)), _const_spec((1, d))],
        out_specs=pl.BlockSpec((1, TM, d), lambda bb, t: (bb, t, 0)),
        compiler_params=_cparams("arbitrary", "arbitrary"),
        name="final_norm",
    )(x2, gain)


def _segment_matrices(width):
    heads = jnp.arange(width) // HEAD_DIM
    seg = (heads[:, None] == jnp.arange(LANES)[None, :]).astype(BF16)
    return seg, seg.T


def _rope_tables(seq, ctx_len):
    t = jnp.arange(seq)
    pos = jnp.stack([(t // GRID_W).astype(F32), (t % GRID_W).astype(F32)], axis=-1)
    half = HEAD_DIM // 4
    inv_freq = ROPE_THETA ** (-jnp.arange(0, 2 * half, 2, dtype=F32) / (2 * half))
    ang = pos[:, :, None] * inv_freq
    cos, sin = jnp.cos(ang), jnp.sin(ang)
    zero = jnp.zeros_like(sin)
    per_head = lambda first, second: jnp.concatenate([first, second], axis=-1).reshape(seq, HEAD_DIM)
    cos_h = per_head(cos, cos)
    sin_lo = per_head(-sin, zero)
    sin_hi = per_head(zero, sin)
    pad = lambda tab, fill: jnp.concatenate([jnp.full((ctx_len, HEAD_DIM), fill, F32), tab], axis=0)
    two = lambda tab: jnp.concatenate([tab, tab], axis=1)
    return two(pad(cos_h, 1.0)), two(pad(sin_lo, 0.0)), two(pad(sin_hi, 0.0))


def kernel(x, c, ctx, c_ctx, mod_w, mod_b, norm_mix, norm_ffn, rwkv_mu, rwkv_w_rkv, rwkv_w0, rwkv_w1, rwkv_w2,
           rwkv_a0, rwkv_a1, rwkv_a2, rwkv_g1, rwkv_g2, rwkv_k_k, rwkv_k_a, rwkv_r_k, rwkv_ln_w, rwkv_ln_b,
           rwkv_w_o, attn_w_qkv, attn_q_gain, attn_k_gain, attn_w_o, moe_router, moe_w_gate_up, moe_w_down,
           final_norm):
    batch, seq, d = x.shape
    ctx_len = ctx.shape[1]
    depth = mod_w.shape[0]
    tt = ctx_len + seq
    tps = tt // TM
    assert ctx_len == TM and seq % TM == 0 and CHUNK == HEAD_DIM and batch + 1 <= 16
    nq = d // HEAD_DIM

    x2 = jnp.concatenate([ctx, x], axis=1).reshape(batch * tt, d)
    cc = jnp.zeros((16, d), F32).at[0].set(c_ctx).at[1:batch + 1].set(c)
    mods = _mod_vectors(cc, mod_w, mod_b).reshape(depth, 16, N_MOD, d)
    seg, segt = _segment_matrices(d)
    segk, segtk = _segment_matrices(ATTN_KV_HEADS * HEAD_DIM)
    rope = _rope_tables(seq, ctx_len)
    cap_l = EC_CAPACITY * seq // N_EXPERTS
    cap_c = EC_CAPACITY * ctx_len // N_EXPERTS
    sets = ((ctx_len, seq, cap_l, 0), (0, ctx_len, cap_c, cap_l))
    router_pad = jnp.zeros((depth, d, LANES), F32).at[:, :, :N_EXPERTS].set(moe_router)

    ia = ib = 0
    for i in range(depth):
        mod = mods[i]
        if i % 2 == 0:
            zpad = jnp.zeros((HEAD_DIM, d), F32)
            lora_pad = lambda w: jnp.stack([jnp.concatenate([w[0], zpad], 0), jnp.concatenate([zpad, w[1]], 0)]).astype(BF16)
            p = {
                "mu": rwkv_mu[ia], "w_rkv": rwkv_w_rkv[ia].astype(BF16),
                "w1": jnp.concatenate([rwkv_w1[ia, 0], rwkv_w1[ia, 1]], axis=1).astype(BF16), "w2": lora_pad(rwkv_w2[ia]),
                "w0": rwkv_w0[ia],
                "a1": jnp.concatenate([rwkv_a1[ia, 0], rwkv_a1[ia, 1]], axis=1).astype(BF16), "a2": lora_pad(rwkv_a2[ia]),
                "a0": rwkv_a0[ia],
                "g1": rwkv_g1[ia].astype(BF16), "g2": rwkv_g2[ia].astype(BF16),
                "k_k": rwkv_k_k[ia].reshape(1, d), "k_a": rwkv_k_a[ia].reshape(1, d),
                "r_k": rwkv_r_k[ia].reshape(1, d), "ln_w": rwkv_ln_w[ia].reshape(1, d), "ln_b": rwkv_ln_b[ia].reshape(1, d),
                "w_o": rwkv_w_o[ia].astype(BF16),
            }
            r, v, kk, g, lw, kd, b = _rwkv_inputs(x2, mod, norm_mix[i].reshape(1, d), p, seg, segt, tps)
            yf, yr = _rwkv_scan(r, v, kk, lw, kd, b, batch, ctx_len)
            x2 = _rwkv_readout(yf, yr, r, kd, v, g, x2, mod, p, seg, segt, tps)
            ia += 1
        else:
            p = {
                "w_qkv": attn_w_qkv[ib].astype(BF16),
                "q_gain": jnp.tile(attn_q_gain[ib], nq).reshape(1, d),
                "k_gain": jnp.tile(attn_k_gain[ib], ATTN_KV_HEADS).reshape(1, ATTN_KV_HEADS * HEAD_DIM),
            }
            q, k, v = _gqa_project(x2, mod, norm_mix[i].reshape(1, d), p, rope, seg, segt, segk, segtk, tps)
            o = _attention(q, k, v, batch, tps, ctx_len)
            x2 = _attn_out(o, x2, mod, attn_w_o[ib].astype(BF16), tps)
            ib += 1
        f, afft = _moe_router(x2, mod, norm_ffn[i].reshape(1, d), router_pad[i], batch, tps)
        layer_sets = sets if i < depth - 1 else sets[:1]
        idx, gate = _moe_select(afft, layer_sets, sum(s[2] for s in layer_sets))
        xe = _moe_gather(idx, f, batch)
        ye = _moe_experts(xe, moe_w_gate_up, moe_w_down, mod[1:batch + 1, 5:6], mod[0, 5:6], i, cap_l)
        x2 = _moe_combine(idx, gate, ye, x2, tt)
    return _final_norm(x2, final_norm.reshape(1, d), batch, tps)
```
